```python
import math, functools
import jax, jax.numpy as jnp
from jax import lax
import numpy as np

D_MODEL = 4096
BATCH = 4
SEQ = 2048
DEPTH = 1
DEC_BATCH = 32
DEC_SEQ = 4
PAST_LEN = 8192
PAGE_SIZE = 128

D_HEAD = 128
H_GDN = (D_MODEL // 2) // D_HEAD
H_SB = (D_MODEL // 2) // D_HEAD
D_GDN = H_GDN * D_HEAD
D_SB = H_SB * D_HEAD
D_MIX = D_GDN + D_SB
CONV_W = 4
GDN_CONV_DIM = 3 * D_GDN
GDN_CHUNK = 64
SB_QBLOCK = 128
SB_BIAS_LO = -9.0
SB_BIAS_HI = -5.0
N_MEM = 256
H_MEM = 4
D_MEM = 128
D_FF = 4 * D_MODEL
EPS = 1e-6
IN_COLS = GDN_CONV_DIM + D_GDN + 2 * H_GDN + 3 * D_SB
DECAY_COL = GDN_CONV_DIM + D_GDN

kernel_name = 'hymba_gdn_stickbreak_step'


def rmsnorm(x, g):
    xf = x.astype(jnp.float32)
    y = xf * lax.rsqrt(jnp.mean(xf * xf, axis=-1, keepdims=True) + EPS)
    return (y * g.astype(jnp.float32)).astype(x.dtype)


def l2norm(x):
    xf = x.astype(jnp.float32)
    return (xf * lax.rsqrt(jnp.sum(xf * xf, axis=-1, keepdims=True) + EPS)).astype(x.dtype)


def split_columns(proj):
    sizes = (GDN_CONV_DIM, D_GDN, H_GDN, H_GDN, D_SB, D_SB, D_SB)
    points = [sum(sizes[:i + 1]) for i in range(len(sizes) - 1)]
    return jnp.split(proj, points, axis=-1)


def causal_conv(x, prev, w):
    t = x.shape[1]
    xp = jnp.concatenate([prev, x], axis=1)
    y = xp[:, 0:t] * w[0]
    for i in range(1, CONV_W):
        y = y + xp[:, i:i + t] * w[i]
    return jax.nn.silu(y), xp[:, -(CONV_W - 1):]


def gated_delta_chunked(q, k, v, g, beta, s0):
    f32 = jnp.float32
    bn, t, h, dk = q.shape
    dv = v.shape[-1]
    pad = (-t) % GDN_CHUNK

    def prep(u):
        u = u.astype(f32)
        u = jnp.pad(u, [(0, 0), (0, pad)] + [(0, 0)] * (u.ndim - 2))
        n = u.shape[1] // GDN_CHUNK
        u = u.reshape((bn, n, GDN_CHUNK) + u.shape[2:])
        return jnp.moveaxis(u, (1, 3), (0, 2))

    qc, kc, vc, gc, bc = prep(q), prep(k), prep(v), prep(g), prep(beta)
    gc = jnp.cumsum(gc, axis=-1)
    idx = jnp.arange(GDN_CHUNK)
    incl = idx[:, None] >= idx[None, :]
    strict = idx[:, None] > idx[None, :]
    decay = jnp.exp(jnp.where(incl, gc[..., :, None] - gc[..., None, :], -jnp.inf))
    kb = kc * bc[..., None]
    vb = vc * bc[..., None]
    m = jnp.where(strict, jnp.einsum('nbhid,nbhjd->nbhij', kb, kc) * decay, 0.0)
    eye = jnp.eye(GDN_CHUNK, dtype=f32)
    tmat = lax.linalg.triangular_solve(eye + m, jnp.broadcast_to(eye, m.shape),
                                       left_side=True, lower=True, unit_diagonal=True)
    u = tmat @ vb
    w = tmat @ (kb * jnp.exp(gc)[..., None])
    qk = jnp.where(incl, jnp.einsum('nbhid,nbhjd->nbhij', qc, kc) * decay, 0.0)
    qg = qc * jnp.exp(gc)[..., None]
    kg = kc * jnp.exp(gc[..., -1:] - gc)[..., None]
    glast = jnp.exp(gc[..., -1])

    def step(s, xs):
        u_i, w_i, qk_i, qg_i, kg_i, gl_i = xs
        v_new = u_i - jnp.einsum('bhck,bhkv->bhcv', w_i, s)
        o_i = jnp.einsum('bhck,bhkv->bhcv', qg_i, s) + jnp.einsum('bhij,bhjv->bhiv', qk_i, v_new)
        s = s * gl_i[..., None, None] + jnp.einsum('bhck,bhcv->bhkv', kg_i, v_new)
        return s, o_i

    s_fin, o = lax.scan(step, s0.astype(f32), (u, w, qk, qg, kg, glast))
    o = jnp.moveaxis(o, (0, 2), (1, 3)).reshape(bn, -1, h, dv)[:, :t]
    return o.astype(v.dtype), s_fin.astype(s0.dtype)


def gdn_mix(qkv, z, a, b, conv_prev, s0, w_conv, a_log, dt_bias, gdn_norm):
    bn, t, _ = qkv.shape
    qkv, conv_new = causal_conv(qkv, conv_prev, w_conv)
    q, k, v = jnp.split(qkv, 3, axis=-1)
    q = l2norm(q.reshape(bn, t, H_GDN, D_HEAD)) * D_HEAD ** -0.5
    k = l2norm(k.reshape(bn, t, H_GDN, D_HEAD))
    v = v.reshape(bn, t, H_GDN, D_HEAD)
    f32 = jnp.float32
    g = -jnp.exp(a_log.astype(f32)) * jax.nn.softplus(a.astype(f32) + dt_bias.astype(f32))
    beta = jax.nn.sigmoid(b.astype(f32))
    o, s_new = gated_delta_chunked(q, k, v, g, beta, s0)
    o = rmsnorm(o, gdn_norm) * jax.nn.silu(z.reshape(bn, t, H_GDN, D_HEAD))
    return o.reshape(bn, t, D_GDN), conv_new, s_new


def sb_weights(z, valid):
    log_stay = jnp.where(valid, jax.nn.log_sigmoid(-z), 0.0)
    after = lax.cumsum(log_stay, axis=z.ndim - 1, reverse=True) - log_stay
    return jnp.exp(jnp.where(valid, jax.nn.log_sigmoid(z) + after, -jnp.inf))


def sb_prompt(q, k, v, bias):
    bn, t, h, d = q.shape
    nb = t // SB_QBLOCK
    qb = jnp.moveaxis(q.reshape(bn, nb, SB_QBLOCK, h, d), 1, 0)
    qpos = jnp.arange(t).reshape(nb, SB_QBLOCK)
    kpos = jnp.arange(t)
    bias = bias.astype(jnp.float32)[:, None, None]

    def blk(args):
        q_i, pos_i = args
        z = jnp.einsum('bqhd,bkhd->bhqk', q_i, k).astype(jnp.float32) * D_HEAD ** -0.5 + bias
        a = sb_weights(z, kpos[None, :] < pos_i[:, None])
        return jnp.einsum('bhqk,bkhd->bqhd', a.astype(v.dtype), v)

    o = lax.map(blk, (qb, qpos))
    return jnp.moveaxis(o, 0, 1).reshape(bn, t, h, d)


def sb_sample(q, k_new, v_new, bias, cache_k, cache_v, page_table):
    bn, t, h, d = q.shape
    k_past = cache_k[page_table].reshape(bn, -1, h, d)
    v_past = cache_v[page_table].reshape(bn, -1, h, d)
    p_len = k_past.shape[1]
    z = jnp.concatenate([jnp.einsum('bqhd,bkhd->bhqk', q, k_past),
                         jnp.einsum('bqhd,bkhd->bhqk', q, k_new)], axis=-1).astype(jnp.float32) * D_HEAD ** -0.5
    z = z + bias.astype(jnp.float32)[:, None, None]
    qpos = p_len + jnp.arange(t)
    kpos = jnp.arange(p_len + t)
    a = sb_weights(z, kpos[None, :] < qpos[:, None]).astype(v_past.dtype)
    return (jnp.einsum('bhqk,bkhd->bqhd', a[..., :p_len], v_past)
            + jnp.einsum('bhqk,bkhd->bqhd', a[..., p_len:], v_new))


def mem_attend(h, mem_k, mem_v, w_q, w_o):
    bn, t, _ = h.shape
    q = (h @ w_q).reshape(bn, t, H_MEM, D_MEM)
    s = jnp.einsum('bthd,bmhd->bhtm', q, mem_k).astype(jnp.float32) * D_MEM ** -0.5
    pr = jax.nn.softmax(s, axis=-1).astype(mem_v.dtype)
    o = jnp.einsum('bhtm,bmhd->bthd', pr, mem_v).reshape(bn, t, H_MEM * D_MEM)
    return o @ w_o


def decoder_layer(x, conv_prev, rec_prev, mem_k, mem_v, sb_fn, p):
    bn, t, _ = x.shape
    h = rmsnorm(x, p['norm_mix'])
    qkv_g, z_g, a_g, b_g, q_s, k_s, v_s = split_columns(h @ p['w_in'])
    o_g, conv_new, rec_new = gdn_mix(qkv_g, z_g, a_g, b_g, conv_prev, rec_prev, p['w_conv'],
                                     p['a_log'], p['dt_bias'], p['gdn_norm'])
    q_s = q_s.reshape(bn, t, H_SB, D_HEAD)
    k_s = k_s.reshape(bn, t, H_SB, D_HEAD)
    v_s = v_s.reshape(bn, t, H_SB, D_HEAD)
    o_s = rmsnorm(sb_fn(q_s, k_s, v_s, p['sb_bias']), p['sb_norm']).reshape(bn, t, D_SB)
    x = x + jnp.concatenate([o_g, o_s], axis=-1) @ p['w_out']
    x = x + mem_attend(rmsnorm(x, p['norm_mem']), mem_k, mem_v, p['w_mem_q'], p['w_mem_o'])
    h = rmsnorm(x, p['norm_mlp'])
    x = x + jnp.square(jax.nn.relu(h @ p['w_up'])) @ p['w_down']
    return x, k_s, v_s, conv_new, rec_new


def setup_inputs(seed: int = 0) -> dict:
    key = jax.random.key(seed)
    ks = jax.random.split(key, 29)
    f32 = jnp.float32
    n_pages = PAST_LEN // PAGE_SIZE
    n_used = DEC_BATCH * n_pages
    n_phys = n_used + n_used // 4

    def nrm(k, shape, scale=1.0):
        return jax.random.normal(k, shape, f32) * scale

    def gain(k, shape):
        return 1.0 + 0.05 * jax.random.normal(k, shape, f32)

    x_prompt = nrm(ks[0], (BATCH, SEQ, D_MODEL))
    x_sample = nrm(ks[1], (DEC_BATCH, DEC_SEQ, D_MODEL))
    cache_sb_k = nrm(ks[2], (DEPTH, n_phys, PAGE_SIZE, H_SB, D_HEAD))
    cache_sb_v = nrm(ks[3], (DEPTH, n_phys, PAGE_SIZE, H_SB, D_HEAD))
    state_gdn_conv = nrm(ks[4], (DEPTH, DEC_BATCH, CONV_W - 1, GDN_CONV_DIM))
    state_gdn_rec = nrm(ks[5], (DEPTH, DEC_BATCH, H_GDN, D_HEAD, D_HEAD), 0.1)
    cache_mem_k = nrm(ks[6], (DEPTH, DEC_BATCH, N_MEM, H_MEM, D_MEM))
    cache_mem_v = nrm(ks[7], (DEPTH, DEC_BATCH, N_MEM, H_MEM, D_MEM))
    page_table = jax.random.permutation(ks[8], n_phys)[:n_used].reshape(DEC_BATCH, n_pages).astype(jnp.int32)
    mem_prompt = nrm(ks[9], (BATCH, N_MEM, D_MODEL))
    norm_mix = gain(ks[10], (DEPTH, D_MODEL))
    w_in = nrm(ks[11], (DEPTH, D_MODEL, IN_COLS), D_MODEL ** -0.5)
    w_in = w_in.at[:, :, DECAY_COL:DECAY_COL + H_GDN].multiply(0.1)
    w_conv = nrm(ks[12], (DEPTH, CONV_W, GDN_CONV_DIM), CONV_W ** -0.5)
    a_log = jnp.log(jax.random.uniform(ks[13], (DEPTH, H_GDN), f32, minval=1.0, maxval=16.0))
    dt = jnp.exp(jax.random.uniform(ks[14], (DEPTH, H_GDN), f32, minval=math.log(1e-3), maxval=math.log(1e-1)))
    dt_bias = dt + jnp.log(-jnp.expm1(-dt))
    gdn_norm = gain(ks[15], (DEPTH, D_HEAD))
    sb_norm = gain(ks[16], (DEPTH, D_HEAD))
    sb_bias = jax.random.uniform(ks[28], (DEPTH, H_SB), f32, minval=SB_BIAS_LO, maxval=SB_BIAS_HI)
    w_out = nrm(ks[17], (DEPTH, D_MIX, D_MODEL), D_MIX ** -0.5)
    norm_mem = gain(ks[18], (DEPTH, D_MODEL))
    mem_norm = gain(ks[19], (DEPTH, D_MODEL))
    w_mem_q = nrm(ks[20], (DEPTH, D_MODEL, H_MEM * D_MEM), D_MODEL ** -0.5)
    w_mem_k = nrm(ks[21], (DEPTH, D_MODEL, H_MEM * D_MEM), D_MODEL ** -0.5)
    w_mem_v = nrm(ks[22], (DEPTH, D_MODEL, H_MEM * D_MEM), D_MODEL ** -0.5)
    w_mem_o = nrm(ks[23], (DEPTH, H_MEM * D_MEM, D_MODEL), (H_MEM * D_MEM) ** -0.5)
    norm_mlp = gain(ks[24], (DEPTH, D_MODEL))
    w_up = nrm(ks[25], (DEPTH, D_MODEL, D_FF), D_MODEL ** -0.5)
    w_down = nrm(ks[26], (DEPTH, D_FF, D_MODEL), D_FF ** -0.5)
    norm_final = gain(ks[27], (D_MODEL,))
    return {'x_prompt': x_prompt, 'x_sample': x_sample,
            'cache_sb_k': cache_sb_k, 'cache_sb_v': cache_sb_v,
            'state_gdn_conv': state_gdn_conv, 'state_gdn_rec': state_gdn_rec,
            'cache_mem_k': cache_mem_k, 'cache_mem_v': cache_mem_v,
            'page_table': page_table, 'mem_prompt': mem_prompt,
            'norm_mix': norm_mix, 'w_in': w_in, 'w_conv': w_conv, 'a_log': a_log, 'dt_bias': dt_bias,
            'gdn_norm': gdn_norm, 'sb_norm': sb_norm, 'sb_bias': sb_bias, 'w_out': w_out,
            'norm_mem': norm_mem, 'mem_norm': mem_norm, 'w_mem_q': w_mem_q, 'w_mem_k': w_mem_k,
            'w_mem_v': w_mem_v, 'w_mem_o': w_mem_o, 'norm_mlp': norm_mlp, 'w_up': w_up,
            'w_down': w_down, 'norm_final': norm_final}


def reference(x_prompt, x_sample, cache_sb_k, cache_sb_v, state_gdn_conv, state_gdn_rec,
              cache_mem_k, cache_mem_v, page_table, mem_prompt,
              norm_mix, w_in, w_conv, a_log, dt_bias, gdn_norm, sb_norm, sb_bias, w_out,
              norm_mem, mem_norm, w_mem_q, w_mem_k, w_mem_v, w_mem_o, norm_mlp, w_up, w_down, norm_final):
    xp, xs = x_prompt, x_sample
    bp = x_prompt.shape[0]
    kp_l, vp_l, ks_l, vs_l = [], [], [], []
    cp_l, cs_l, rp_l, rs_l, mk_l, mv_l = [], [], [], [], [], []
    for l in range(DEPTH):
        p = {'norm_mix': norm_mix[l], 'w_in': w_in[l], 'w_conv': w_conv[l], 'a_log': a_log[l],
             'dt_bias': dt_bias[l], 'gdn_norm': gdn_norm[l], 'sb_norm': sb_norm[l], 'sb_bias': sb_bias[l],
             'w_out': w_out[l], 'norm_mem': norm_mem[l], 'w_mem_q': w_mem_q[l], 'w_mem_o': w_mem_o[l],
             'norm_mlp': norm_mlp[l], 'w_up': w_up[l], 'w_down': w_down[l]}
        memn = rmsnorm(mem_prompt, mem_norm[l])
        mk = (memn @ w_mem_k[l]).reshape(bp, N_MEM, H_MEM, D_MEM)
        mv = (memn @ w_mem_v[l]).reshape(bp, N_MEM, H_MEM, D_MEM)
        conv0 = jnp.zeros((bp, CONV_W - 1, GDN_CONV_DIM), xp.dtype)
        rec0 = jnp.zeros((bp, H_GDN, D_HEAD, D_HEAD), state_gdn_rec.dtype)
        xp, kp, vp, cp, rp = decoder_layer(xp, conv0, rec0, mk, mv, sb_prompt, p)
        sb_fn = functools.partial(sb_sample, cache_k=cache_sb_k[l], cache_v=cache_sb_v[l], page_table=page_table)
        xs, ks_, vs_, cs, rs = decoder_layer(xs, state_gdn_conv[l], state_gdn_rec[l],
                                             cache_mem_k[l], cache_mem_v[l], sb_fn, p)
        kp_l.append(kp); vp_l.append(vp); ks_l.append(ks_); vs_l.append(vs_)
        cp_l.append(cp); cs_l.append(cs); rp_l.append(rp); rs_l.append(rs)
        mk_l.append(mk); mv_l.append(mv)
    y_prompt = rmsnorm(xp, norm_final)
    y_sample = rmsnorm(xs, norm_final)
    return (y_prompt, y_sample, jnp.stack(kp_l), jnp.stack(vp_l), jnp.stack(ks_l), jnp.stack(vs_l),
            jnp.stack(cp_l), jnp.stack(cs_l), jnp.stack(rp_l), jnp.stack(rs_l), jnp.stack(mk_l), jnp.stack(mv_l))
```

```python
import functools
import math

import jax
import jax.numpy as jnp
from jax import lax
from jax.experimental import pallas as pl
from jax.experimental.pallas import tpu as pltpu

F32 = jnp.float32
BF16 = jnp.bfloat16

EPS = 1e-6
D_HEAD = 128
SUBLANES = 8
GDN_CHUNK = 64
GDN_HEADS_PER_STEP = 4
GDN_PROMPT_ROWS = 256
CONV_W = 4
PAGE_SIZE = 128
SB_TQ = 256
SB_TK = 128
V7X_VMEM_LIMIT_BYTES = 56 * 1024 * 1024
MM_TN = 512
MM_TM = 640
ROW_TILE = 128


def _cparams(sem):
    return pltpu.CompilerParams(dimension_semantics=sem, vmem_limit_bytes=V7X_VMEM_LIMIT_BYTES)


def _softplus(x):
    return jnp.maximum(x, 0.0) + jnp.log1p(jnp.exp(-jnp.abs(x)))


def _sigmoid(x):
    return 1.0 / (1.0 + jnp.exp(-x))


def _dot(a, b):
    return jnp.dot(a, b, preferred_element_type=F32)


def _dot_nt(a, b):
    return lax.dot_general(a, b, (((1,), (1,)), ((), ())), preferred_element_type=F32)


def _split2(x):
    hi = x.astype(BF16)
    lo = (x - hi.astype(F32)).astype(BF16)
    return hi, lo


def _split3(x):
    hi = x.astype(BF16)
    r = x - hi.astype(F32)
    mid = r.astype(BF16)
    lo = (r - mid.astype(F32)).astype(BF16)
    return hi, mid, lo


def _dot_hp(a, b):
    a_hi, a_lo = _split2(a)
    b_hi, b_lo = _split2(b)
    return _dot(a_hi, b_hi) + (_dot(a_hi, b_lo) + _dot(a_lo, b_hi))


def _rms(x, g):
    y = x * lax.rsqrt(jnp.mean(x * x, axis=-1, keepdims=True) + EPS)
    return y * g


def _rmsnorm_body(x_ref, g_ref, o_ref):
    o_ref[...] = _rms(x_ref[...], g_ref[...]).astype(o_ref.dtype)


def _rmsnorm(x, g, out_dtype, tm, row_block_offset=0, rows=None):
    m, d = x.shape
    rows = m if rows is None else rows
    off = row_block_offset
    assert rows % tm == 0
    return pl.pallas_call(
        _rmsnorm_body,
        out_shape=jax.ShapeDtypeStruct((rows, d), out_dtype),
        grid=(rows // tm,),
        in_specs=[pl.BlockSpec((tm, d), lambda i: (i + off, 0)),
                  pl.BlockSpec((1, d), lambda i: (0, 0))],
        out_specs=pl.BlockSpec((tm, d), lambda i: (i, 0)),
        compiler_params=_cparams(("parallel",)),
        name="rmsnorm",
    )(x, g.reshape(1, d))


def _rmsnorm_join_body(xp_ref, xs_ref, g_ref, h_ref, x_ref, *, n_prompt_blocks):
    i = pl.program_id(0)

    @pl.when(i < n_prompt_blocks)
    def _():
        x = xp_ref[...]
        x_ref[...] = x
        h_ref[...] = _rms(x, g_ref[...]).astype(h_ref.dtype)

    @pl.when(i >= n_prompt_blocks)
    def _():
        x = xs_ref[...]
        x_ref[...] = x
        h_ref[...] = _rms(x, g_ref[...]).astype(h_ref.dtype)


def _rmsnorm_join(xp, xs, g):
    mp, d = xp.shape
    ms = xs.shape[0]
    tm = ms
    assert mp % tm == 0
    npb = mp // tm
    return pl.pallas_call(
        functools.partial(_rmsnorm_join_body, n_prompt_blocks=npb),
        out_shape=(jax.ShapeDtypeStruct((mp + ms, d), BF16),
                   jax.ShapeDtypeStruct((mp + ms, d), F32)),
        grid=(npb + 1,),
        in_specs=[pl.BlockSpec((tm, d), lambda i: (jnp.minimum(i, npb - 1), 0)),
                  pl.BlockSpec((tm, d), lambda i: (0, 0)),
                  pl.BlockSpec((1, d), lambda i: (0, 0))],
        out_specs=(pl.BlockSpec((tm, d), lambda i: (i, 0)),
                   pl.BlockSpec((tm, d), lambda i: (i, 0))),
        compiler_params=_cparams(("arbitrary",)),
        name="rmsnorm_join",
    )(xp, xs, g.reshape(1, d))


def _matmul_body(*refs, relu2, has_res):
    if has_res:
        a_ref, w_ref, r_ref, o_ref, wbf_ref = refs
    else:
        a_ref, w_ref, o_ref, wbf_ref = refs

    @pl.when(pl.program_id(1) == 0)
    def _():
        wbf_ref[...] = w_ref[...].astype(BF16)

    acc = _dot(a_ref[...], wbf_ref[...])
    if relu2:
        acc = jnp.square(jnp.maximum(acc, 0.0))
    if has_res:
        acc = acc + r_ref[...]
    o_ref[...] = acc.astype(o_ref.dtype)


def _matmul(a, w, n, *, tm, k=None, tn=MM_TN, a_col_block=0, w_row_block=0, w_col_block=0,
            res=None, relu2=False, out_dtype=F32, name="matmul"):
    m = a.shape[0]
    k = a.shape[1] if k is None else k
    while n % tn:
        tn //= 2
    assert m % tm == 0 and tn % D_HEAD == 0
    in_specs = [pl.BlockSpec((tm, k), lambda j, i: (i, a_col_block)),
                pl.BlockSpec((k, tn), lambda j, i: (w_row_block, j + w_col_block))]
    args = [a, w]
    if res is not None:
        in_specs.append(pl.BlockSpec((tm, tn), lambda j, i: (i, j)))
        args.append(res)
    return pl.pallas_call(
        functools.partial(_matmul_body, relu2=relu2, has_res=res is not None),
        out_shape=jax.ShapeDtypeStruct((m, n), out_dtype),
        grid=(n // tn, m // tm),
        in_specs=in_specs,
        out_specs=pl.BlockSpec((tm, tn), lambda j, i: (i, j)),
        scratch_shapes=[pltpu.VMEM((k, tn), BF16)],
        compiler_params=_cparams(("parallel", "arbitrary")),
        name=name,
    )(*args)


def _mem_attn_body(q_ref, k_ref, v_ref, o_ref, *, n_heads, scale):
    for h in range(n_heads):
        sl = slice(h * D_HEAD, (h + 1) * D_HEAD)
        q = q_ref[0, :, sl].astype(BF16)
        k = k_ref[0, :, sl].astype(BF16)
        v = v_ref[0, :, sl].astype(BF16)
        s = _dot_nt(q, k) * scale
        s = s - jnp.max(s, axis=-1, keepdims=True)
        e = jnp.exp(s)
        p = e / jnp.sum(e, axis=-1, keepdims=True)
        o_ref[0, :, sl] = _dot(p.astype(BF16), v).astype(o_ref.dtype)


def _mem_attn(q, k, v, tq, out_dtype):
    nb, t, hd = q.shape
    n_mem = k.shape[1]
    return pl.pallas_call(
        functools.partial(_mem_attn_body, n_heads=hd // D_HEAD, scale=D_HEAD ** -0.5),
        out_shape=jax.ShapeDtypeStruct((nb, t, hd), out_dtype),
        grid=(nb, t // tq),
        in_specs=[pl.BlockSpec((1, tq, hd), lambda b, i: (b, i, 0)),
                  pl.BlockSpec((1, n_mem, hd), lambda b, i: (b, 0, 0)),
                  pl.BlockSpec((1, n_mem, hd), lambda b, i: (b, 0, 0))],
        out_specs=pl.BlockSpec((1, tq, hd), lambda b, i: (b, i, 0)),
        compiler_params=_cparams(("parallel", "parallel")),
        name="mem_attn",
    )(q, k, v)


def _strict_upper_ones(n):
    r = lax.broadcasted_iota(jnp.int32, (n, n), 0)
    c = lax.broadcasted_iota(jnp.int32, (n, n), 1)
    return jnp.where(r > c, 1.0, 0.0).astype(BF16)


def _sb_block(z, valid, c, u):
    sp = _softplus(z)
    ls = -sp if valid is None else jnp.where(valid, -sp, 0.0)
    hi, lo = _split2(ls)
    after = _dot(hi, u) + _dot(lo, u)
    a = jnp.exp((z - sp) + after + c)
    if valid is not None:
        a = jnp.where(valid, a, 0.0)
    return a, c + jnp.sum(ls, axis=-1, keepdims=True)


def _sb_prompt_body(bias_ref, q_ref, k_ref, v_ref, g_ref, mix_ref, o_ref, *, tq, tk, scale):
    del mix_ref
    h = pl.program_id(1)
    qi = pl.program_id(2)
    r = tq // tk
    bias = bias_ref[h]
    q = q_ref[...].astype(BF16)
    u = _strict_upper_ones(tk)
    row = lax.broadcasted_iota(jnp.int32, (tq, tk), 0)
    col = lax.broadcasted_iota(jnp.int32, (tq, tk), 1)

    def block(j, c, acc, masked):
        k0 = pl.multiple_of(j * tk, tk)
        kb = k_ref[pl.ds(k0, tk), :].astype(BF16)
        vb = v_ref[pl.ds(k0, tk), :].astype(BF16)
        z = _dot_nt(q, kb) * scale + bias
        valid = ((k0 + col) < (qi * tq + row)) if masked else None
        a, c = _sb_block(z, valid, c, u)
        return c, acc + _dot(a.astype(BF16), vb)

    c = jnp.zeros((tq, 1), F32)
    acc = jnp.zeros((tq, D_HEAD), F32)
    for i in range(r):
        c, acc = block(qi * r + (r - 1 - i), c, acc, True)

    def body(it, carry):
        c, acc = carry
        for i in range(r):
            c, acc = block((qi - 1 - it) * r + (r - 1 - i), c, acc, False)
        return c, acc

    c, acc = lax.fori_loop(0, qi, body, (c, acc))
    o_ref[...] = _rms(acc, g_ref[...]).astype(o_ref.dtype)


def _sb_prompt(qkv, mix, bias, gain, *, nb, t, n_heads, q_col, k_col, v_col, o_col):
    tq, tk = SB_TQ, SB_TK
    nq = t // tq
    return pl.pallas_call(
        functools.partial(_sb_prompt_body, tq=tq, tk=tk, scale=D_HEAD ** -0.5),
        out_shape=jax.ShapeDtypeStruct(mix.shape, mix.dtype),
        grid=(nb, n_heads, nq),
        in_specs=[pl.BlockSpec(memory_space=pltpu.SMEM),
                  pl.BlockSpec((tq, D_HEAD), lambda b, h, i: (b * nq + i, q_col + h)),
                  pl.BlockSpec((t, D_HEAD), lambda b, h, i: (b, k_col + h)),
                  pl.BlockSpec((t, D_HEAD), lambda b, h, i: (b, v_col + h)),
                  pl.BlockSpec((1, D_HEAD), lambda b, h, i: (0, 0)),
                  pl.BlockSpec(memory_space=pl.ANY)],
        out_specs=pl.BlockSpec((tq, D_HEAD), lambda b, h, i: (b * nq + i, o_col + h)),
        input_output_aliases={5: 0},
        compiler_params=_cparams(("parallel", "parallel", "arbitrary")),
        name="sb_prompt",
    )(bias, qkv, qkv, qkv, gain.reshape(1, D_HEAD), mix)


def _sb_sample_body(pt_ref, q_ref, kn_ref, vn_ref, ck_ref, cv_ref, bias_ref, g_ref, o_ref,
                    qblk_ref, c_ref, acc_ref, new_ref, *, t, n_heads, n_pages, scale):
    del pt_ref
    s = pl.program_id(1)
    n_pairs = n_heads // 2
    rp = 2 * SUBLANES
    rows = n_pairs * rp
    tk = PAGE_SIZE
    u = _strict_upper_ones(tk)

    def process(get_k, get_v, valid):
        zs = [_dot_nt(qblk_ref[g].astype(BF16), get_k(g)) for g in range(n_pairs)]
        z = jnp.concatenate(zs, axis=0) * scale + bias_ref[...]
        a, c_new = _sb_block(z, valid, c_ref[...], u)
        c_ref[...] = c_new
        a = a.astype(BF16)
        for g in range(n_pairs):
            sl = slice(g * rp, (g + 1) * rp)
            acc_ref[sl, :] += _dot(a[sl, :], get_v(g))

    @pl.when(s == 0)
    def _():
        qblk_ref[...] = jnp.zeros(qblk_ref.shape, F32)
        for g in range(n_pairs):
            for hh in range(2):
                hd = 2 * g + hh
                qblk_ref[g, hh * SUBLANES:hh * SUBLANES + t, hh * D_HEAD:(hh + 1) * D_HEAD] = (
                    q_ref[0, :, hd * D_HEAD:(hd + 1) * D_HEAD])
        c_ref[...] = jnp.zeros(c_ref.shape, F32)
        acc_ref[...] = jnp.zeros(acc_ref.shape, F32)
        new_ref[...] = jnp.zeros(new_ref.shape, F32)
        new_ref[0, 0:t, :] = kn_ref[0]
        new_ref[1, 0:t, :] = vn_ref[0]
        qrow = lax.broadcasted_iota(jnp.int32, (rows, tk), 0) % SUBLANES
        col = lax.broadcasted_iota(jnp.int32, (rows, tk), 1)
        valid = jnp.logical_and(col < qrow, qrow < t)
        process(lambda g: new_ref[0, :, g * 2 * D_HEAD:(g + 1) * 2 * D_HEAD].astype(BF16),
                lambda g: new_ref[1, :, g * 2 * D_HEAD:(g + 1) * 2 * D_HEAD].astype(BF16),
                valid)

    @pl.when(s > 0)
    def _():
        def pair(ref, g):
            a = ref[0, pl.ds(2 * g, tk, stride=n_heads), :]
            b = ref[0, pl.ds(2 * g + 1, tk, stride=n_heads), :]
            return jnp.concatenate([a, b], axis=1).astype(BF16)

        process(lambda g: pair(ck_ref, g), lambda g: pair(cv_ref, g), None)

    @pl.when(s == n_pages)
    def _():
        for g in range(n_pairs):
            for hh in range(2):
                hd = 2 * g + hh
                r0 = g * rp + hh * SUBLANES
                o = acc_ref[r0:r0 + t, hh * D_HEAD:(hh + 1) * D_HEAD]
                o_ref[0, :, hd * D_HEAD:(hd + 1) * D_HEAD] = _rms(o, g_ref[...])


def _sb_sample(q, k_new, v_new, cache_k, cache_v, page_table, bias, gain):
    nb, t, hd = q.shape
    n_heads = hd // D_HEAD
    n_pages = page_table.shape[1]
    assert t <= SUBLANES and n_heads % 2 == 0
    rows = n_heads * SUBLANES
    bias_rows = jnp.repeat(bias.astype(F32), SUBLANES).reshape(rows, 1)

    def page_idx(b, s, pt):
        return (pt[b, n_pages - jnp.maximum(s, 1)], 0, 0)

    grid_spec = pltpu.PrefetchScalarGridSpec(
        num_scalar_prefetch=1,
        grid=(nb, n_pages + 1),
        in_specs=[pl.BlockSpec((1, t, hd), lambda b, s, pt: (b, 0, 0)),
                  pl.BlockSpec((1, t, hd), lambda b, s, pt: (b, 0, 0)),
                  pl.BlockSpec((1, t, hd), lambda b, s, pt: (b, 0, 0)),
                  pl.BlockSpec((1, PAGE_SIZE * n_heads, D_HEAD), page_idx),
                  pl.BlockSpec((1, PAGE_SIZE * n_heads, D_HEAD), page_idx),
                  pl.BlockSpec((rows, 1), lambda b, s, pt: (0, 0)),
                  pl.BlockSpec((1, D_HEAD), lambda b, s, pt: (0, 0))],
        out_specs=pl.BlockSpec((1, t, hd), lambda b, s, pt: (b, 0, 0)),
        scratch_shapes=[pltpu.VMEM((n_heads // 2, 2 * SUBLANES, 2 * D_HEAD), F32),
                        pltpu.VMEM((rows, 1), F32),
                        pltpu.VMEM((rows, 2 * D_HEAD), F32),
                        pltpu.VMEM((2, PAGE_SIZE, hd), F32)],
    )
    return pl.pallas_call(
        functools.partial(_sb_sample_body, t=t, n_heads=n_heads, n_pages=n_pages,
                          scale=D_HEAD ** -0.5),
        out_shape=jax.ShapeDtypeStruct((nb, t, hd), F32),
        grid_spec=grid_spec,
        compiler_params=_cparams(("parallel", "arbitrary")),
        name="sb_sample",
    )(page_table, q, k_new, v_new, cache_k, cache_v, bias_rows, gain.reshape(1, D_HEAD))


def _unit_lower_inverse(m, eye):
    n = -m
    t = eye + n
    p = n
    for _ in range(int(math.log2(m.shape[0])) - 1):
        p = _dot_hp(p, p)
        t = t + _dot_hp(t, p)
    return t


def _gdn_body(*refs, tb, tv, sample, n_tb):
    g_heads = GDN_HEADS_PER_STEP
    gw = g_heads * D_HEAD
    c_len = GDN_CHUNK
    if sample:
        (xq_ref, xk_ref, xv_ref, z_ref, pq_ref, pk_ref, pv_ref, ab_ref, wq_ref, wk_ref, wv_ref,
         par_ref, gn_ref, s0_ref, o_ref, rec_ref, ext_ref, qkv_ref, gate_ref, s_ref) = refs
    else:
        (xq_ref, xk_ref, xv_ref, z_ref, pq_ref, pk_ref, pv_ref, ab_ref, wq_ref, wk_ref, wv_ref,
         par_ref, gn_ref, o_ref, rec_ref, ext_ref, qkv_ref, gate_ref, s_ref) = refs
    ti = pl.program_id(2)

    @pl.when(ti == 0)
    def _():
        if sample:
            s_ref[...] = s0_ref[0]
        else:
            s_ref[...] = jnp.zeros(s_ref.shape, F32)

    for part, (x_ref, p_ref, w_ref) in enumerate(((xq_ref, pq_ref, wq_ref),
                                                   (xk_ref, pk_ref, wk_ref),
                                                   (xv_ref, pv_ref, wv_ref))):
        if sample:
            ext_ref[part, 0:SUBLANES, :] = jnp.zeros((SUBLANES, gw), F32)
            ext_ref[part, SUBLANES - (CONV_W - 1):SUBLANES, :] = p_ref[0]
            ext_ref[part, SUBLANES:SUBLANES + tb, :] = jnp.zeros((tb, gw), F32)
            ext_ref[part, SUBLANES:SUBLANES + tv, :] = x_ref[0]
        else:
            ext_ref[part, 0:SUBLANES, :] = jnp.where(ti > 0, p_ref[...], 0.0)
            ext_ref[part, SUBLANES:SUBLANES + tb, :] = x_ref[...]
        w = w_ref[...]
        base = SUBLANES - (CONV_W - 1)
        y = ext_ref[part, base:base + tb, :] * w[0:1, :]
        for i in range(1, CONV_W):
            y = y + ext_ref[part, base + i:base + i + tb, :] * w[i:i + 1, :]
        y = y * _sigmoid(y)
        if part < 2:
            for g in range(g_heads):
                sl = slice(g * D_HEAD, (g + 1) * D_HEAD)
                seg = y[:, sl]
                seg = seg * lax.rsqrt(jnp.sum(seg * seg, axis=-1, keepdims=True) + EPS)
                if part == 0:
                    seg = seg * (D_HEAD ** -0.5)
                qkv_ref[part, :, sl] = seg
        else:
            qkv_ref[part] = y

    if sample:
        gate_ref[0] = jnp.zeros((tb, D_HEAD), F32)
        gate_ref[0, 0:tv, :] = ab_ref[0]
        ab = gate_ref[0]
    else:
        ab = ab_ref[...]
    a_log = par_ref[0, 0:1, :]
    dt_bias = par_ref[0, 1:2, :]
    rowi = lax.broadcasted_iota(jnp.int32, (tb, D_HEAD), 0)
    lane = lax.broadcasted_iota(jnp.int32, (tb, D_HEAD), 1)
    live = rowi < tv
    gdec = -jnp.exp(a_log) * _softplus(ab + dt_bias)
    gate_ref[0] = jnp.where(jnp.logical_and(live, lane < g_heads), gdec, 0.0)
    gate_ref[1] = jnp.where(live, _sigmoid(ab), 0.0)

    ri = lax.broadcasted_iota(jnp.int32, (c_len, c_len), 0)
    ci = lax.broadcasted_iota(jnp.int32, (c_len, c_len), 1)
    incl = ri >= ci
    strict = ri > ci
    eye = jnp.where(ri == ci, 1.0, 0.0).astype(F32)
    tri = jnp.where(ri <= ci, 1.0, 0.0).astype(BF16)
    gn = gn_ref[...]

    def chunk(ck, carry):
        r0 = pl.multiple_of(ck * c_len, c_len)
        rs = pl.ds(r0, c_len)
        g_t = gate_ref[0, rs, :].T
        hi, mid, lo = _split3(g_t)
        gc_t = _dot(hi, tri) + (_dot(mid, tri) + _dot(lo, tri))
        gc = gc_t.T
        beta_all = gate_ref[1, rs, :]
        for g in range(g_heads):
            sl = slice(g * D_HEAD, (g + 1) * D_HEAD)
            gcol = gc[:, g:g + 1]
            grow = gc_t[g:g + 1, :]
            glast = gcol[c_len - 1:c_len, :]
            beta = beta_all[:, g_heads + g:g_heads + g + 1]
            q = qkv_ref[0, rs, sl]
            k = qkv_ref[1, rs, sl]
            v = qkv_ref[2, rs, sl]
            decay = jnp.exp(jnp.where(incl, gcol - grow, -jnp.inf))
            eg = jnp.exp(gcol)
            kb = k * beta
            vb = v * beta
            k16 = k.astype(BF16)
            m = jnp.where(strict, _dot_nt(kb.astype(BF16), k16) * decay, 0.0)
            qk = jnp.where(incl, _dot_nt(q.astype(BF16), k16) * decay, 0.0)
            tmat = _unit_lower_inverse(m, eye)
            rhs = jnp.concatenate([vb, kb * eg], axis=1).astype(BF16)
            uw = _dot(tmat.astype(BF16), rhs)
            u_i = uw[:, :D_HEAD]
            w_i = uw[:, D_HEAD:]
            s = s_ref[g]
            s16 = s.astype(BF16)
            ws = _dot(jnp.concatenate([w_i, q * eg], axis=0).astype(BF16), s16)
            v_new = u_i - ws[:c_len]
            v16 = v_new.astype(BF16)
            o = ws[c_len:] + _dot(qk.astype(BF16), v16)
            kg_t = (k * jnp.exp(glast - gcol)).T.astype(BF16)
            s_ref[g] = s * jnp.exp(glast) + _dot(kg_t, v16)
            o = _rms(o, gn)
            if sample:
                zz = z_ref[0, :, sl]
                o_ref[0, :, sl] = o[0:tv] * (zz * _sigmoid(zz))
            else:
                zz = z_ref[rs, sl]
                o_ref[rs, sl] = (o * (zz * _sigmoid(zz))).astype(o_ref.dtype)
        return carry

    lax.fori_loop(0, tb // c_len, chunk, 0)

    @pl.when(ti == n_tb - 1)
    def _():
        rec_ref[0] = s_ref[...]


def _gdn_scratch(tb):
    g = GDN_HEADS_PER_STEP
    gw = g * D_HEAD
    return [pltpu.VMEM((3, tb + SUBLANES, gw), F32),
            pltpu.VMEM((3, tb, gw), F32),
            pltpu.VMEM((2, tb, D_HEAD), F32),
            pltpu.VMEM((g, D_HEAD, D_HEAD), F32)]


def _gdn_params(a_log, dt_bias, n_heads):
    g = GDN_HEADS_PER_STEP
    ng = n_heads // g
    par = jnp.zeros((ng, SUBLANES, D_HEAD), F32)
    par = par.at[:, 0, :g].set(a_log.reshape(ng, g).astype(F32))
    par = par.at[:, 1, :g].set(dt_bias.reshape(ng, g).astype(F32))
    return par


def _gdn_prompt(qkvz, gates, gate_col, w_conv, par, gn, *, nb, t, n_heads):
    g = GDN_HEADS_PER_STEP
    gw = g * D_HEAD
    ng = n_heads // g
    tb = GDN_PROMPT_ROWS
    n_tb = t // tb
    hd = n_heads * D_HEAD
    m = qkvz.shape[0]

    def cur(part):
        return pl.BlockSpec((tb, gw), lambda b, hg, ti: (b * n_tb + ti, part * ng + hg))

    def prev(part):
        return pl.BlockSpec(
            (SUBLANES, gw),
            lambda b, hg, ti: (jnp.maximum((b * n_tb + ti) * (tb // SUBLANES) - 1, 0), part * ng + hg))

    def wspec(part):
        return pl.BlockSpec((CONV_W, gw), lambda b, hg, ti: (0, part * ng + hg))

    in_specs = [cur(0), cur(1), cur(2), cur(3), prev(0), prev(1), prev(2),
                pl.BlockSpec((tb, D_HEAD), lambda b, hg, ti: (b * n_tb + ti, gate_col + hg)),
                wspec(0), wspec(1), wspec(2),
                pl.BlockSpec((1, SUBLANES, D_HEAD), lambda b, hg, ti: (hg, 0, 0)),
                pl.BlockSpec((1, D_HEAD), lambda b, hg, ti: (0, 0))]
    out_shape = (jax.ShapeDtypeStruct((m, 2 * hd), BF16),
                 jax.ShapeDtypeStruct((nb, n_heads, D_HEAD, D_HEAD), F32))
    out_specs = (pl.BlockSpec((tb, gw), lambda b, hg, ti: (b * n_tb + ti, hg)),
                 pl.BlockSpec((1, g, D_HEAD, D_HEAD), lambda b, hg, ti: (b, hg, 0, 0)))
    return pl.pallas_call(
        functools.partial(_gdn_body, tb=tb, tv=tb, sample=False, n_tb=n_tb),
        out_shape=out_shape,
        grid=(nb, ng, n_tb),
        in_specs=in_specs,
        out_specs=out_specs,
        scratch_shapes=_gdn_scratch(tb),
        compiler_params=_cparams(("parallel", "parallel", "arbitrary")),
        name="gdn_prompt",
    )(qkvz, qkvz, qkvz, qkvz, qkvz, qkvz, qkvz, gates, w_conv, w_conv, w_conv,
      par, gn.reshape(1, D_HEAD))


def _gdn_sample(x, conv_prev, ab, rec_prev, w_conv, par, gn, *, n_heads):
    nb, t, _ = x.shape
    g = GDN_HEADS_PER_STEP
    gw = g * D_HEAD
    ng = n_heads // g
    tb = GDN_CHUNK
    hd = n_heads * D_HEAD
    assert t <= tb

    def cur(part):
        return pl.BlockSpec((1, t, gw), lambda b, hg, ti: (b, 0, part * ng + hg))

    def prev(part):
        return pl.BlockSpec((1, CONV_W - 1, gw), lambda b, hg, ti: (b, 0, part * ng + hg))

    def wspec(part):
        return pl.BlockSpec((CONV_W, gw), lambda b, hg, ti: (0, part * ng + hg))

    in_specs = [cur(0), cur(1), cur(2), cur(3), prev(0), prev(1), prev(2),
                pl.BlockSpec((1, t, D_HEAD), lambda b, hg, ti: (b, 0, hg)),
                wspec(0), wspec(1), wspec(2),
                pl.BlockSpec((1, SUBLANES, D_HEAD), lambda b, hg, ti: (hg, 0, 0)),
                pl.BlockSpec((1, D_HEAD), lambda b, hg, ti: (0, 0)),
                pl.BlockSpec((1, g, D_HEAD, D_HEAD), lambda b, hg, ti: (b, hg, 0, 0))]
    out_shape = (jax.ShapeDtypeStruct((nb, t, hd), F32),
                 jax.ShapeDtypeStruct((nb, n_heads, D_HEAD, D_HEAD), F32))
    out_specs = (pl.BlockSpec((1, t, gw), lambda b, hg, ti: (b, 0, hg)),
                 pl.BlockSpec((1, g, D_HEAD, D_HEAD), lambda b, hg, ti: (b, hg, 0, 0)))
    return pl.pallas_call(
        functools.partial(_gdn_body, tb=tb, tv=t, sample=True, n_tb=1),
        out_shape=out_shape,
        grid=(nb, ng, 1),
        in_specs=in_specs,
        out_specs=out_specs,
        scratch_shapes=_gdn_scratch(tb),
        compiler_params=_cparams(("parallel", "parallel", "arbitrary")),
        name="gdn_sample",
    )(x, x, x, x, conv_prev, conv_prev, conv_prev, ab, w_conv, w_conv, w_conv,
      par, gn.reshape(1, D_HEAD), rec_prev)


def _layer(l, x, h, cache_sb_k, cache_sb_v, state_gdn_conv, state_gdn_rec, cache_mem_k,
           cache_mem_v, page_table, mem_prompt, w_in, w_conv, a_log, dt_bias, gdn_norm,
           sb_norm, sb_bias, w_out, norm_mem, mem_norm, w_mem_q, w_mem_k, w_mem_v, w_mem_o,
           norm_mlp, w_up, w_down, *, bp, tp, bs, ts):
    m, d_model = x.shape
    mp, ms = bp * tp, bs * ts
    n_heads = w_conv.shape[-1] // (3 * D_HEAD)
    hd = n_heads * D_HEAD
    n_sb = sb_bias.shape[-1]
    sbd = n_sb * D_HEAD
    gh = GDN_HEADS_PER_STEP
    ng = n_heads // gh
    tm = MM_TM if m % MM_TM == 0 else ROW_TILE

    w_in_l = w_in[l]
    qkvz = _matmul(h, w_in_l, 4 * hd, tm=tm, name="in_proj_gdn")
    ab_col = 4 * hd
    w_a = w_in_l[:, ab_col:ab_col + n_heads].reshape(d_model, ng, gh)
    w_b = w_in_l[:, ab_col + n_heads:ab_col + 2 * n_heads].reshape(d_model, ng, gh)
    w_ab = jnp.concatenate([w_a, w_b, jnp.zeros((d_model, ng, D_HEAD - 2 * gh), w_in_l.dtype)],
                           axis=-1).reshape(d_model, ng * D_HEAD)
    w_rest = jnp.concatenate([w_in_l[:, ab_col + 2 * n_heads:], w_ab], axis=1)
    qkv_sb = _matmul(h, w_rest, 3 * sbd + ng * D_HEAD, tm=tm, name="in_proj_sb")

    par = _gdn_params(a_log[l], dt_bias[l], n_heads)
    mix, rec_p = _gdn_prompt(qkvz, qkv_sb, 3 * n_sb, w_conv[l], par, gdn_norm[l],
                             nb=bp, t=tp, n_heads=n_heads)
    xs_rows = qkvz[mp:].reshape(bs, ts, 4 * hd)
    ab_s = qkv_sb[mp:, 3 * sbd:].reshape(bs, ts, ng * D_HEAD)
    og_s, rec_s = _gdn_sample(xs_rows, state_gdn_conv[l], ab_s, state_gdn_rec[l], w_conv[l], par,
                              gdn_norm[l], n_heads=n_heads)
    conv_p = qkvz[:mp].reshape(bp, tp, 4 * hd)[:, tp - (CONV_W - 1):, :3 * hd]
    conv_s = jnp.concatenate([state_gdn_conv[l], xs_rows[:, :, :3 * hd]], axis=1)[:, -(CONV_W - 1):]

    mix = _sb_prompt(qkv_sb, mix, sb_bias[l].astype(F32), sb_norm[l], nb=bp, t=tp, n_heads=n_sb,
                     q_col=0, k_col=n_sb, v_col=2 * n_sb, o_col=n_heads)
    samp = qkv_sb[mp:, :3 * sbd]
    q_s = samp[:, :sbd].reshape(bs, ts, sbd)
    k_s = samp[:, sbd:2 * sbd].reshape(bs, ts, sbd)
    v_s = samp[:, 2 * sbd:].reshape(bs, ts, sbd)
    n_phys = cache_sb_k.shape[1]
    ck = cache_sb_k[l].reshape(n_phys, PAGE_SIZE * n_sb, D_HEAD)
    cv = cache_sb_v[l].reshape(n_phys, PAGE_SIZE * n_sb, D_HEAD)
    os_s = _sb_sample(q_s, k_s, v_s, ck, cv, page_table, sb_bias[l], sb_norm[l])
    mix_s = jnp.concatenate([og_s.reshape(ms, hd), os_s.reshape(ms, sbd)], axis=1).astype(BF16)
    mix = lax.dynamic_update_slice(mix, mix_s, (mp, 0))
    k_p = qkv_sb[:mp, sbd:2 * sbd].reshape(bp, tp, n_sb, D_HEAD)
    v_p = qkv_sb[:mp, 2 * sbd:3 * sbd].reshape(bp, tp, n_sb, D_HEAD)

    x = _matmul(mix, w_out[l], d_model, tm=tm, res=x, name="out_proj")

    n_mem = mem_prompt.shape[1]
    md = w_mem_k.shape[-1]
    memn = _rmsnorm(mem_prompt.reshape(bp * n_mem, d_model), mem_norm[l], BF16, n_mem)
    mk = _matmul(memn, w_mem_k[l], md, tm=bp * n_mem, name="mem_k_proj")
    mv = _matmul(memn, w_mem_v[l], md, tm=bp * n_mem, name="mem_v_proj")
    h = _rmsnorm(x, norm_mem[l], BF16, ROW_TILE)
    qm = _matmul(h, w_mem_q[l], md, tm=tm, out_dtype=BF16, name="mem_q_proj")
    om_p = _mem_attn(qm[:mp].reshape(bp, tp, md), mk.reshape(bp, n_mem, md), mv.reshape(bp, n_mem, md),
                     min(tp, 512), BF16)
    qm_s = jnp.pad(qm[mp:].astype(F32).reshape(bs, ts, md), ((0, 0), (0, SUBLANES - ts), (0, 0)))
    om_s = _mem_attn(qm_s, cache_mem_k[l].reshape(bs, n_mem, md), cache_mem_v[l].reshape(bs, n_mem, md),
                     SUBLANES, F32)[:, :ts]
    om = jnp.concatenate([om_p.reshape(mp, md), om_s.reshape(ms, md).astype(BF16)], axis=0)
    x = _matmul(om, w_mem_o[l], d_model, tm=tm, res=x, name="mem_o_proj")

    h = _rmsnorm(x, norm_mlp[l], BF16, ROW_TILE)
    d_ff = w_up.shape[-1]
    hid = _matmul(h, w_up[l], d_ff, tm=tm, relu2=True, out_dtype=BF16, name="mlp_up")
    half = d_ff // 2
    x = _matmul(hid, w_down[l], d_model, tm=tm, k=half, tn=MM_TN // 2, res=x, name="mlp_down_lo")
    x = _matmul(hid, w_down[l], d_model, tm=tm, k=half, tn=MM_TN // 2, a_col_block=1, w_row_block=1,
                res=x, name="mlp_down_hi")

    n_mh = md // D_HEAD
    outs = dict(k_p=k_p, v_p=v_p, k_s=k_s.reshape(bs, ts, n_sb, D_HEAD),
                v_s=v_s.reshape(bs, ts, n_sb, D_HEAD), conv_p=conv_p, conv_s=conv_s,
                rec_p=rec_p, rec_s=rec_s,
                mk=mk.reshape(bp, n_mem, n_mh, D_HEAD), mv=mv.reshape(bp, n_mem, n_mh, D_HEAD))
    return x, outs


def kernel(x_prompt, x_sample, cache_sb_k, cache_sb_v, state_gdn_conv, state_gdn_rec, cache_mem_k,
           cache_mem_v, page_table, mem_prompt, norm_mix, w_in, w_conv, a_log, dt_bias, gdn_norm,
           sb_norm, sb_bias, w_out, norm_mem, mem_norm, w_mem_q, w_mem_k, w_mem_v, w_mem_o,
           norm_mlp, w_up, w_down, norm_final):
    bp, tp, d_model = x_prompt.shape
    bs, ts, _ = x_sample.shape
    mp, ms = bp * tp, bs * ts
    depth = w_in.shape[0]
    h, x = _rmsnorm_join(x_prompt.reshape(mp, d_model), x_sample.reshape(ms, d_model), norm_mix[0])
    per_layer = []
    for l in range(depth):
        if l > 0:
            h = _rmsnorm(x, norm_mix[l], BF16, ROW_TILE)
        x, outs = _layer(
            l, x, h, cache_sb_k, cache_sb_v, state_gdn_conv, state_gdn_rec, cache_mem_k,
            cache_mem_v, page_table, mem_prompt, w_in, w_conv, a_log, dt_bias, gdn_norm,
            sb_norm, sb_bias, w_out, norm_mem, mem_norm, w_mem_q, w_mem_k, w_mem_v, w_mem_o,
            norm_mlp, w_up, w_down, bp=bp, tp=tp, bs=bs, ts=ts)
        per_layer.append(outs)
    y_p = _rmsnorm(x, norm_final, F32, ROW_TILE, rows=mp).reshape(bp, tp, d_model)
    y_s = _rmsnorm(x, norm_final, F32, ms, row_block_offset=mp // ms, rows=ms).reshape(bs, ts, d_model)

    def stack(name):
        return jnp.stack([o[name] for o in per_layer])

    return (y_p, y_s, stack("k_p"), stack("v_p"), stack("k_s"), stack("v_s"),
            stack("conv_p"), stack("conv_s"), stack("rec_p"), stack("rec_s"),
            stack("mk"), stack("mv"))
```

```python
import functools
import math

import jax
import jax.numpy as jnp
from jax import lax
from jax.experimental import pallas as pl
from jax.experimental.pallas import tpu as pltpu

F32 = jnp.float32
BF16 = jnp.bfloat16

EPS = 1e-6
D_HEAD = 128
SUBLANES = 8
GDN_CHUNK = 64
GDN_HEADS_PER_STEP = 4
GDN_PROMPT_ROWS = 256
CONV_W = 4
PAGE_SIZE = 128
SB_TQ = 512
SB_TK = 128
SB_PAGES_PER_STEP = 4
V7X_VMEM_LIMIT_BYTES = 56 * 1024 * 1024
MM_TN = 512
MM_TM = 640
ROW_TILE = 128


def _cparams(sem):
    return pltpu.CompilerParams(dimension_semantics=sem, vmem_limit_bytes=V7X_VMEM_LIMIT_BYTES)


def _softplus(x):
    return jnp.maximum(x, 0.0) + jnp.log1p(jnp.exp(-jnp.abs(x)))


def _sigmoid(x):
    return 1.0 / (1.0 + jnp.exp(-x))


def _dot(a, b):
    return jnp.dot(a, b, preferred_element_type=F32)


def _dot_nt(a, b):
    return lax.dot_general(a, b, (((1,), (1,)), ((), ())), preferred_element_type=F32)


def _split2(x):
    hi = x.astype(BF16)
    lo = (x - hi.astype(F32)).astype(BF16)
    return hi, lo


def _split3(x):
    hi = x.astype(BF16)
    r = x - hi.astype(F32)
    mid = r.astype(BF16)
    lo = (r - mid.astype(F32)).astype(BF16)
    return hi, mid, lo


def _dot_hp(a, b):
    a_hi, a_lo = _split2(a)
    b_hi, b_lo = _split2(b)
    return _dot(a_hi, b_hi) + (_dot(a_hi, b_lo) + _dot(a_lo, b_hi))


def _rms(x, g):
    y = x * lax.rsqrt(jnp.mean(x * x, axis=-1, keepdims=True) + EPS)
    return y * g


def _rmsnorm_body(x_ref, g_ref, o_ref):
    o_ref[...] = _rms(x_ref[...], g_ref[...]).astype(o_ref.dtype)


def _rmsnorm(x, g, out_dtype, tm, row_block_offset=0, rows=None):
    m, d = x.shape
    rows = m if rows is None else rows
    off = row_block_offset
    assert rows % tm == 0
    return pl.pallas_call(
        _rmsnorm_body,
        out_shape=jax.ShapeDtypeStruct((rows, d), out_dtype),
        grid=(rows // tm,),
        in_specs=[pl.BlockSpec((tm, d), lambda i: (i + off, 0)),
                  pl.BlockSpec((1, d), lambda i: (0, 0))],
        out_specs=pl.BlockSpec((tm, d), lambda i: (i, 0)),
        compiler_params=_cparams(("parallel",)),
        name="rmsnorm",
    )(x, g.reshape(1, d))


def _rmsnorm_join_body(xp_ref, xs_ref, g_ref, h_ref, x_ref, *, n_prompt_blocks):
    i = pl.program_id(0)

    @pl.when(i < n_prompt_blocks)
    def _():
        x = xp_ref[...]
        x_ref[...] = x
        h_ref[...] = _rms(x, g_ref[...]).astype(h_ref.dtype)

    @pl.when(i >= n_prompt_blocks)
    def _():
        x = xs_ref[...]
        x_ref[...] = x
        h_ref[...] = _rms(x, g_ref[...]).astype(h_ref.dtype)


def _rmsnorm_join(xp, xs, g):
    mp, d = xp.shape
    ms = xs.shape[0]
    tm = ms
    assert mp % tm == 0
    npb = mp // tm
    return pl.pallas_call(
        functools.partial(_rmsnorm_join_body, n_prompt_blocks=npb),
        out_shape=(jax.ShapeDtypeStruct((mp + ms, d), BF16),
                   jax.ShapeDtypeStruct((mp + ms, d), F32)),
        grid=(npb + 1,),
        in_specs=[pl.BlockSpec((tm, d), lambda i: (jnp.minimum(i, npb - 1), 0)),
                  pl.BlockSpec((tm, d), lambda i: (0, 0)),
                  pl.BlockSpec((1, d), lambda i: (0, 0))],
        out_specs=(pl.BlockSpec((tm, d), lambda i: (i, 0)),
                   pl.BlockSpec((tm, d), lambda i: (i, 0))),
        compiler_params=_cparams(("arbitrary",)),
        name="rmsnorm_join",
    )(xp, xs, g.reshape(1, d))


def _matmul_body(*refs, relu2, has_res):
    if has_res:
        a_ref, w_ref, r_ref, o_ref, wbf_ref = refs
    else:
        a_ref, w_ref, o_ref, wbf_ref = refs

    @pl.when(pl.program_id(1) == 0)
    def _():
        wbf_ref[...] = w_ref[...].astype(BF16)

    acc = _dot(a_ref[...], wbf_ref[...])
    if relu2:
        acc = jnp.square(jnp.maximum(acc, 0.0))
    if has_res:
        acc = acc + r_ref[...]
    o_ref[...] = acc.astype(o_ref.dtype)


def _matmul(a, w, n, *, tm, k=None, tn=MM_TN, a_col_block=0, w_row_block=0, w_col_block=0,
            res=None, relu2=False, out_dtype=F32, name="matmul"):
    m = a.shape[0]
    k = a.shape[1] if k is None else k
    while n % tn:
        tn //= 2
    assert m % tm == 0 and tn % D_HEAD == 0
    in_specs = [pl.BlockSpec((tm, k), lambda j, i: (i, a_col_block)),
                pl.BlockSpec((k, tn), lambda j, i: (w_row_block, j + w_col_block))]
    args = [a, w]
    if res is not None:
        in_specs.append(pl.BlockSpec((tm, tn), lambda j, i: (i, j)))
        args.append(res)
    return pl.pallas_call(
        functools.partial(_matmul_body, relu2=relu2, has_res=res is not None),
        out_shape=jax.ShapeDtypeStruct((m, n), out_dtype),
        grid=(n // tn, m // tm),
        in_specs=in_specs,
        out_specs=pl.BlockSpec((tm, tn), lambda j, i: (i, j)),
        scratch_shapes=[pltpu.VMEM((k, tn), BF16)],
        compiler_params=_cparams(("parallel", "arbitrary")),
        name=name,
    )(*args)


def _matmul_nt_body(a_ref, w_ref, o_ref, wbf_ref):
    @pl.when(pl.program_id(1) == 0)
    def _():
        wbf_ref[...] = w_ref[...].astype(BF16)

    o_ref[...] = _dot_nt(a_ref[...], wbf_ref[...]).astype(o_ref.dtype)


def _matmul_nt(a, wt, row_segments, *, tm, tn=MM_TN, name="matmul_nt"):
    m, k = a.shape
    starts = []
    for first, n_rows in row_segments:
        assert first % SUBLANES == 0 and n_rows % tn == 0
        starts += [first + i * tn for i in range(n_rows // tn)]
    n = tn * len(starts)
    def w_row(j):
        row8 = j * (tn // SUBLANES)
        base, shift = 0, 0
        for first, n_rows in row_segments:
            step = (first - base * tn) - shift
            if step:
                row8 = row8 + jnp.where(j >= base, step // SUBLANES, 0)
            shift += step
            base += n_rows // tn
        return row8 * SUBLANES

    return pl.pallas_call(
        _matmul_nt_body,
        out_shape=jax.ShapeDtypeStruct((m, n), F32),
        grid=(n // tn, m // tm),
        in_specs=[pl.BlockSpec((tm, k), lambda j, i: (i, 0)),
                  pl.BlockSpec((pl.Element(tn), pl.Element(k)), lambda j, i: (w_row(j), 0))],
        out_specs=pl.BlockSpec((tm, tn), lambda j, i: (i, j)),
        scratch_shapes=[pltpu.VMEM((tn, k), BF16)],
        compiler_params=_cparams(("parallel", "arbitrary")),
        name=name,
    )(a, wt)


def _mem_attn_body(q_ref, k_ref, v_ref, o_ref, *, n_heads, scale):
    for h in range(n_heads):
        sl = slice(h * D_HEAD, (h + 1) * D_HEAD)
        q = q_ref[0, :, sl].astype(BF16)
        k = k_ref[0, :, sl].astype(BF16)
        v = v_ref[0, :, sl].astype(BF16)
        s = _dot_nt(q, k) * scale
        s = s - jnp.max(s, axis=-1, keepdims=True)
        e = jnp.exp(s)
        p = e / jnp.sum(e, axis=-1, keepdims=True)
        o_ref[0, :, sl] = _dot(p.astype(BF16), v).astype(o_ref.dtype)


def _mem_attn(q, k, v, tq, out_dtype):
    nb, t, hd = q.shape
    n_mem = k.shape[1]
    return pl.pallas_call(
        functools.partial(_mem_attn_body, n_heads=hd // D_HEAD, scale=D_HEAD ** -0.5),
        out_shape=jax.ShapeDtypeStruct((nb, t, hd), out_dtype),
        grid=(nb, t // tq),
        in_specs=[pl.BlockSpec((1, tq, hd), lambda b, i: (b, i, 0)),
                  pl.BlockSpec((1, n_mem, hd), lambda b, i: (b, 0, 0)),
                  pl.BlockSpec((1, n_mem, hd), lambda b, i: (b, 0, 0))],
        out_specs=pl.BlockSpec((1, tq, hd), lambda b, i: (b, i, 0)),
        compiler_params=_cparams(("parallel", "parallel")),
        name="mem_attn",
    )(q, k, v)


def _strict_upper_ones(n):
    r = lax.broadcasted_iota(jnp.int32, (n, n), 0)
    c = lax.broadcasted_iota(jnp.int32, (n, n), 1)
    return jnp.where(r > c, 1.0, 0.0).astype(BF16)


def _sb_blocks(zs, valids, c, u):
    sps = [_softplus(z) for z in zs]
    lss = [-sp if v is None else jnp.where(v, -sp, 0.0) for sp, v in zip(sps, valids)]
    parts = [_split2(ls) for ls in lss]
    afters = [_dot(hi, u) + _dot(lo, u) for hi, lo in parts]
    out = []
    for z, sp, ls, after, v in zip(zs, sps, lss, afters, valids):
        a = jnp.exp((z - sp) + after + c)
        out.append(a if v is None else jnp.where(v, a, 0.0))
        c = c + jnp.sum(ls, axis=-1, keepdims=True)
    return out, c


def _sb_prompt_body(bias_ref, q_ref, k_ref, v_ref, g_ref, mix_ref, o_ref, *, tq, tk, scale):
    del mix_ref
    h = pl.program_id(1)
    qi = pl.program_id(2)
    r = tq // tk
    bias = bias_ref[h]
    q = q_ref[...].astype(BF16)
    u = _strict_upper_ones(tk)
    row = lax.broadcasted_iota(jnp.int32, (tq, tk), 0)
    col = lax.broadcasted_iota(jnp.int32, (tq, tk), 1)

    def blocks(j_hi, c, acc, masked):
        k0s = [pl.multiple_of((j_hi - i) * tk, tk) for i in range(r)]
        zs = [_dot_nt(q, k_ref[pl.ds(k0, tk), :].astype(BF16)) * scale + bias for k0 in k0s]
        valids = [((k0 + col) < (qi * tq + row)) if masked else None for k0 in k0s]
        ws, c = _sb_blocks(zs, valids, c, u)
        for w, k0 in zip(ws, k0s):
            acc = acc + _dot(w.astype(BF16), v_ref[pl.ds(k0, tk), :].astype(BF16))
        return c, acc

    c = jnp.zeros((tq, 1), F32)
    acc = jnp.zeros((tq, D_HEAD), F32)
    c, acc = blocks(qi * r + (r - 1), c, acc, True)

    def body(it, carry):
        return blocks((qi - 1 - it) * r + (r - 1), *carry, False)

    c, acc = lax.fori_loop(0, qi, body, (c, acc))
    o_ref[...] = _rms(acc, g_ref[...]).astype(o_ref.dtype)


def _sb_prompt(qkv, mix, bias, gain, *, nb, t, n_heads, q_col, k_col, v_col, o_col):
    tq, tk = SB_TQ, SB_TK
    nq = t // tq
    return pl.pallas_call(
        functools.partial(_sb_prompt_body, tq=tq, tk=tk, scale=D_HEAD ** -0.5),
        out_shape=jax.ShapeDtypeStruct(mix.shape, mix.dtype),
        grid=(nb, n_heads, nq),
        in_specs=[pl.BlockSpec(memory_space=pltpu.SMEM),
                  pl.BlockSpec((tq, D_HEAD), lambda b, h, i: (b * nq + i, q_col + h)),
                  pl.BlockSpec((t, D_HEAD), lambda b, h, i: (b, k_col + h)),
                  pl.BlockSpec((t, D_HEAD), lambda b, h, i: (b, v_col + h)),
                  pl.BlockSpec((1, D_HEAD), lambda b, h, i: (0, 0)),
                  pl.BlockSpec(memory_space=pl.ANY)],
        out_specs=pl.BlockSpec((tq, D_HEAD), lambda b, h, i: (b * nq + i, o_col + h)),
        input_output_aliases={5: 0},
        compiler_params=_cparams(("parallel", "parallel", "arbitrary")),
        name="sb_prompt",
    )(bias, qkv, qkv, qkv, gain.reshape(1, D_HEAD), mix)


def _sb_sample_body(pt_ref, q_ref, kn_ref, vn_ref, *rest, t, n_heads, n_steps, pages_per_step, scale):
    del pt_ref
    pps = pages_per_step
    ck_refs = rest[:pps]
    cv_refs = rest[pps:2 * pps]
    bias_ref, g_ref, o_ref, qblk_ref, c_ref, acc_ref, new_ref = rest[2 * pps:]
    s = pl.program_id(1)
    n_pairs = n_heads // 2
    rp = 2 * SUBLANES
    rows = n_pairs * rp
    tk = PAGE_SIZE
    u = _strict_upper_ones(tk)

    def process(get_ks, get_vs, valids):
        qb = [qblk_ref[g].astype(BF16) for g in range(n_pairs)]
        zs = [jnp.concatenate([_dot_nt(qb[g], get_k(g)) for g in range(n_pairs)], axis=0) * scale
              + bias_ref[...] for get_k in get_ks]
        ws, c_new = _sb_blocks(zs, valids, c_ref[...], u)
        ws = [w.astype(BF16) for w in ws]
        outs = []
        for g in range(n_pairs):
            sl = slice(g * rp, (g + 1) * rp)
            o = acc_ref[sl, :]
            for w, get_v in zip(ws, get_vs):
                o = o + _dot(w[sl, :], get_v(g))
            outs.append(o)
        c_ref[...] = c_new
        acc_ref[...] = jnp.concatenate(outs, axis=0)

    @pl.when(s == 0)
    def _():
        qblk_ref[...] = jnp.zeros(qblk_ref.shape, F32)
        for g in range(n_pairs):
            for hh in range(2):
                hd = 2 * g + hh
                qblk_ref[g, hh * SUBLANES:hh * SUBLANES + t, hh * D_HEAD:(hh + 1) * D_HEAD] = (
                    q_ref[0, :, hd * D_HEAD:(hd + 1) * D_HEAD])
        c_ref[...] = jnp.zeros(c_ref.shape, F32)
        acc_ref[...] = jnp.zeros(acc_ref.shape, F32)
        new_ref[...] = jnp.zeros(new_ref.shape, F32)
        new_ref[0, 0:t, :] = kn_ref[0]
        new_ref[1, 0:t, :] = vn_ref[0]
        qrow = lax.broadcasted_iota(jnp.int32, (rows, tk), 0) % SUBLANES
        col = lax.broadcasted_iota(jnp.int32, (rows, tk), 1)
        valid = jnp.logical_and(col < qrow, qrow < t)
        process([lambda g: new_ref[0, :, g * 2 * D_HEAD:(g + 1) * 2 * D_HEAD].astype(BF16)],
                [lambda g: new_ref[1, :, g * 2 * D_HEAD:(g + 1) * 2 * D_HEAD].astype(BF16)],
                [valid])

    @pl.when(s > 0)
    def _():
        def pair(ref):
            def get(g):
                a = ref[0, pl.ds(2 * g, tk, stride=n_heads), :]
                b = ref[0, pl.ds(2 * g + 1, tk, stride=n_heads), :]
                return jnp.concatenate([a, b], axis=1).astype(BF16)
            return get

        process([pair(r) for r in ck_refs], [pair(r) for r in cv_refs], [None] * pps)

    @pl.when(s == n_steps - 1)
    def _():
        for g in range(n_pairs):
            for hh in range(2):
                hd = 2 * g + hh
                r0 = g * rp + hh * SUBLANES
                o = acc_ref[r0:r0 + t, hh * D_HEAD:(hh + 1) * D_HEAD]
                o_ref[0, :, hd * D_HEAD:(hd + 1) * D_HEAD] = _rms(o, g_ref[...])


def _sb_sample(q, k_new, v_new, cache_k, cache_v, page_table, bias, gain):
    nb, t, hd = q.shape
    n_heads = hd // D_HEAD
    n_pages = page_table.shape[1]
    assert t <= SUBLANES and n_heads % 2 == 0
    rows = n_heads * SUBLANES
    bias_rows = jnp.repeat(bias.astype(F32), SUBLANES).reshape(rows, 1)

    pps = SB_PAGES_PER_STEP
    while n_pages % pps:
        pps //= 2
    n_steps = n_pages // pps + 1

    def page_spec(p):
        def idx(b, s, pt):
            return (pt[b, n_pages - 1 - ((jnp.maximum(s, 1) - 1) * pps + p)], 0, 0)
        return pl.BlockSpec((1, PAGE_SIZE * n_heads, D_HEAD), idx)

    tok_spec = pl.BlockSpec((1, t, hd), lambda b, s, pt: (b, 0, 0))
    grid_spec = pltpu.PrefetchScalarGridSpec(
        num_scalar_prefetch=1,
        grid=(nb, n_steps),
        in_specs=[tok_spec, tok_spec, tok_spec]
                 + [page_spec(p) for p in range(pps)] + [page_spec(p) for p in range(pps)]
                 + [pl.BlockSpec((rows, 1), lambda b, s, pt: (0, 0)),
                    pl.BlockSpec((1, D_HEAD), lambda b, s, pt: (0, 0))],
        out_specs=pl.BlockSpec((1, t, hd), lambda b, s, pt: (b, 0, 0)),
        scratch_shapes=[pltpu.VMEM((n_heads // 2, 2 * SUBLANES, 2 * D_HEAD), F32),
                        pltpu.VMEM((rows, 1), F32),
                        pltpu.VMEM((rows, 2 * D_HEAD), F32),
                        pltpu.VMEM((2, PAGE_SIZE, hd), F32)],
    )
    return pl.pallas_call(
        functools.partial(_sb_sample_body, t=t, n_heads=n_heads, n_steps=n_steps,
                          pages_per_step=pps, scale=D_HEAD ** -0.5),
        out_shape=jax.ShapeDtypeStruct((nb, t, hd), F32),
        grid_spec=grid_spec,
        compiler_params=_cparams(("parallel", "arbitrary")),
        name="sb_sample",
    )(page_table, q, k_new, v_new, *([cache_k] * pps), *([cache_v] * pps), bias_rows,
      gain.reshape(1, D_HEAD))


def _unit_lower_inverses(ms, eye):
    ps = [-m for m in ms]
    ts = [eye + p for p in ps]
    for _ in range(int(math.log2(ms[0].shape[0])) - 1):
        ps = [_dot_hp(p, p) for p in ps]
        ts = [t + _dot_hp(t, p) for t, p in zip(ts, ps)]
    return ts


def _gdn_body(*refs, tb, tv, sample, n_tb):
    g_heads = GDN_HEADS_PER_STEP
    gw = g_heads * D_HEAD
    c_len = GDN_CHUNK
    if sample:
        (xq_ref, xk_ref, xv_ref, z_ref, pq_ref, pk_ref, pv_ref, ab_ref, wq_ref, wk_ref, wv_ref,
         par_ref, gn_ref, s0_ref, o_ref, rec_ref, ext_ref, qkv_ref, gate_ref, s_ref) = refs
    else:
        (xq_ref, xk_ref, xv_ref, z_ref, pq_ref, pk_ref, pv_ref, ab_ref, wq_ref, wk_ref, wv_ref,
         par_ref, gn_ref, o_ref, rec_ref, ext_ref, qkv_ref, gate_ref, s_ref) = refs
    ti = pl.program_id(2)

    @pl.when(ti == 0)
    def _():
        if sample:
            s_ref[...] = s0_ref[0]
        else:
            s_ref[...] = jnp.zeros(s_ref.shape, F32)

    for part, (x_ref, p_ref, w_ref) in enumerate(((xq_ref, pq_ref, wq_ref),
                                                   (xk_ref, pk_ref, wk_ref),
                                                   (xv_ref, pv_ref, wv_ref))):
        if sample:
            ext_ref[part, 0:SUBLANES, :] = jnp.zeros((SUBLANES, gw), F32)
            ext_ref[part, SUBLANES - (CONV_W - 1):SUBLANES, :] = p_ref[0]
            ext_ref[part, SUBLANES:SUBLANES + tb, :] = jnp.zeros((tb, gw), F32)
            ext_ref[part, SUBLANES:SUBLANES + tv, :] = x_ref[0]
        else:
            ext_ref[part, 0:SUBLANES, :] = jnp.where(ti > 0, p_ref[...], 0.0)
            ext_ref[part, SUBLANES:SUBLANES + tb, :] = x_ref[...]
        w = w_ref[...]
        base = SUBLANES - (CONV_W - 1)
        y = ext_ref[part, base:base + tb, :] * w[0:1, :]
        for i in range(1, CONV_W):
            y = y + ext_ref[part, base + i:base + i + tb, :] * w[i:i + 1, :]
        y = y * _sigmoid(y)
        if part < 2:
            for g in range(g_heads):
                sl = slice(g * D_HEAD, (g + 1) * D_HEAD)
                seg = y[:, sl]
                seg = seg * lax.rsqrt(jnp.sum(seg * seg, axis=-1, keepdims=True) + EPS)
                if part == 0:
                    seg = seg * (D_HEAD ** -0.5)
                qkv_ref[part, :, sl] = seg
        else:
            qkv_ref[part] = y

    if sample:
        gate_ref[0] = jnp.zeros((tb, D_HEAD), F32)
        gate_ref[0, 0:tv, :] = ab_ref[0]
        ab = gate_ref[0]
    else:
        ab = ab_ref[...]
    a_log = par_ref[0, 0:1, :]
    dt_bias = par_ref[0, 1:2, :]
    rowi = lax.broadcasted_iota(jnp.int32, (tb, D_HEAD), 0)
    lane = lax.broadcasted_iota(jnp.int32, (tb, D_HEAD), 1)
    live = rowi < tv
    gdec = -jnp.exp(a_log) * _softplus(ab + dt_bias)
    gate_ref[0] = jnp.where(jnp.logical_and(live, lane < g_heads), gdec, 0.0)
    gate_ref[1] = jnp.where(live, _sigmoid(ab), 0.0)

    ri = lax.broadcasted_iota(jnp.int32, (c_len, c_len), 0)
    ci = lax.broadcasted_iota(jnp.int32, (c_len, c_len), 1)
    incl = ri >= ci
    strict = ri > ci
    eye = jnp.where(ri == ci, 1.0, 0.0).astype(F32)
    tri = jnp.where(ri <= ci, 1.0, 0.0).astype(BF16)
    gn = gn_ref[...]

    def chunk(ck, carry):
        r0 = pl.multiple_of(ck * c_len, c_len)
        rs = pl.ds(r0, c_len)
        g_t = gate_ref[0, rs, :].T
        hi, mid, lo = _split3(g_t)
        gc_t = _dot(hi, tri) + (_dot(mid, tri) + _dot(lo, tri))
        gc = gc_t.T
        beta_all = gate_ref[1, rs, :]
        hs = range(g_heads)
        sls = [slice(g * D_HEAD, (g + 1) * D_HEAD) for g in hs]
        q = [qkv_ref[0, rs, sl] for sl in sls]
        k = [qkv_ref[1, rs, sl] for sl in sls]
        v = [qkv_ref[2, rs, sl] for sl in sls]
        s_old = [s_ref[g] for g in hs]
        if sample:
            zz = [z_ref[0, :, sl] for sl in sls]
        else:
            zz = [z_ref[rs, sl] for sl in sls]
        gcol = [gc[:, g:g + 1] for g in hs]
        grow = [gc_t[g:g + 1, :] for g in hs]
        glast = [gcol[g][c_len - 1:c_len, :] for g in hs]
        beta = [beta_all[:, g_heads + g:g_heads + g + 1] for g in hs]
        decay = [jnp.exp(jnp.where(incl, gcol[g] - grow[g], -jnp.inf)) for g in hs]
        eg = [jnp.exp(gcol[g]) for g in hs]
        kb = [k[g] * beta[g] for g in hs]
        vb = [v[g] * beta[g] for g in hs]
        k16 = [k[g].astype(BF16) for g in hs]
        m = [jnp.where(strict, _dot_nt(kb[g].astype(BF16), k16[g]) * decay[g], 0.0) for g in hs]
        qk = [jnp.where(incl, _dot_nt(q[g].astype(BF16), k16[g]) * decay[g], 0.0) for g in hs]
        tmat = _unit_lower_inverses(m, eye)
        rhs = [jnp.concatenate([vb[g], kb[g] * eg[g]], axis=1).astype(BF16) for g in hs]
        uw = [_dot(tmat[g].astype(BF16), rhs[g]) for g in hs]
        s16 = [s_old[g].astype(BF16) for g in hs]
        ws = [_dot(jnp.concatenate([uw[g][:, D_HEAD:], q[g] * eg[g]], axis=0).astype(BF16), s16[g])
              for g in hs]
        v_new = [uw[g][:, :D_HEAD] - ws[g][:c_len] for g in hs]
        v16 = [v_new[g].astype(BF16) for g in hs]
        o = [ws[g][c_len:] + _dot(qk[g].astype(BF16), v16[g]) for g in hs]
        kg_t = [(k[g] * jnp.exp(glast[g] - gcol[g])).T.astype(BF16) for g in hs]
        s_new = [s_old[g] * jnp.exp(glast[g]) + _dot(kg_t[g], v16[g]) for g in hs]
        out = [_rms(o[g], gn) for g in hs]
        if sample:
            out = [out[g][0:tv] * (zz[g] * _sigmoid(zz[g])) for g in hs]
            o_ref[0] = jnp.concatenate(out, axis=1)
        else:
            out = [out[g] * (zz[g] * _sigmoid(zz[g])) for g in hs]
            o_ref[rs, :] = jnp.concatenate(out, axis=1).astype(o_ref.dtype)
        for g in hs:
            s_ref[g] = s_new[g]
        return carry

    lax.fori_loop(0, tb // c_len, chunk, 0)

    @pl.when(ti == n_tb - 1)
    def _():
        rec_ref[0] = s_ref[...]


def _gdn_scratch(tb):
    g = GDN_HEADS_PER_STEP
    gw = g * D_HEAD
    return [pltpu.VMEM((3, tb + SUBLANES, gw), F32),
            pltpu.VMEM((3, tb, gw), F32),
            pltpu.VMEM((2, tb, D_HEAD), F32),
            pltpu.VMEM((g, D_HEAD, D_HEAD), F32)]


def _gdn_params(a_log, dt_bias, n_heads):
    g = GDN_HEADS_PER_STEP
    ng = n_heads // g
    par = jnp.zeros((ng, SUBLANES, D_HEAD), F32)
    par = par.at[:, 0, :g].set(a_log.reshape(ng, g).astype(F32))
    par = par.at[:, 1, :g].set(dt_bias.reshape(ng, g).astype(F32))
    return par


def _gdn_prompt(qkvz, gates, gate_col, w_conv, par, gn, *, nb, t, n_heads):
    g = GDN_HEADS_PER_STEP
    gw = g * D_HEAD
    ng = n_heads // g
    tb = GDN_PROMPT_ROWS
    n_tb = t // tb
    hd = n_heads * D_HEAD
    m = qkvz.shape[0]

    def cur(part):
        return pl.BlockSpec((tb, gw), lambda b, hg, ti: (b * n_tb + ti, part * ng + hg))

    def prev(part):
        return pl.BlockSpec(
            (SUBLANES, gw),
            lambda b, hg, ti: (jnp.maximum((b * n_tb + ti) * (tb // SUBLANES) - 1, 0), part * ng + hg))

    def wspec(part):
        return pl.BlockSpec((CONV_W, gw), lambda b, hg, ti: (0, part * ng + hg))

    in_specs = [cur(0), cur(1), cur(2), cur(3), prev(0), prev(1), prev(2),
                pl.BlockSpec((tb, D_HEAD), lambda b, hg, ti: (b * n_tb + ti, gate_col + hg)),
                wspec(0), wspec(1), wspec(2),
                pl.BlockSpec((1, SUBLANES, D_HEAD), lambda b, hg, ti: (hg, 0, 0)),
                pl.BlockSpec((1, D_HEAD), lambda b, hg, ti: (0, 0))]
    out_shape = (jax.ShapeDtypeStruct((m, 2 * hd), BF16),
                 jax.ShapeDtypeStruct((nb, n_heads, D_HEAD, D_HEAD), F32))
    out_specs = (pl.BlockSpec((tb, gw), lambda b, hg, ti: (b * n_tb + ti, hg)),
                 pl.BlockSpec((1, g, D_HEAD, D_HEAD), lambda b, hg, ti: (b, hg, 0, 0)))
    return pl.pallas_call(
        functools.partial(_gdn_body, tb=tb, tv=tb, sample=False, n_tb=n_tb),
        out_shape=out_shape,
        grid=(nb, ng, n_tb),
        in_specs=in_specs,
        out_specs=out_specs,
        scratch_shapes=_gdn_scratch(tb),
        compiler_params=_cparams(("parallel", "parallel", "arbitrary")),
        name="gdn_prompt",
    )(qkvz, qkvz, qkvz, qkvz, qkvz, qkvz, qkvz, gates, w_conv, w_conv, w_conv,
      par, gn.reshape(1, D_HEAD))


def _gdn_sample(x, conv_prev, ab, rec_prev, w_conv, par, gn, *, n_heads):
    nb, t, _ = x.shape
    g = GDN_HEADS_PER_STEP
    gw = g * D_HEAD
    ng = n_heads // g
    tb = GDN_CHUNK
    hd = n_heads * D_HEAD
    assert t <= tb

    def cur(part):
        return pl.BlockSpec((1, t, gw), lambda b, hg, ti: (b, 0, part * ng + hg))

    def prev(part):
        return pl.BlockSpec((1, CONV_W - 1, gw), lambda b, hg, ti: (b, 0, part * ng + hg))

    def wspec(part):
        return pl.BlockSpec((CONV_W, gw), lambda b, hg, ti: (0, part * ng + hg))

    in_specs = [cur(0), cur(1), cur(2), cur(3), prev(0), prev(1), prev(2),
                pl.BlockSpec((1, t, D_HEAD), lambda b, hg, ti: (b, 0, hg)),
                wspec(0), wspec(1), wspec(2),
                pl.BlockSpec((1, SUBLANES, D_HEAD), lambda b, hg, ti: (hg, 0, 0)),
                pl.BlockSpec((1, D_HEAD), lambda b, hg, ti: (0, 0)),
                pl.BlockSpec((1, g, D_HEAD, D_HEAD), lambda b, hg, ti: (b, hg, 0, 0))]
    out_shape = (jax.ShapeDtypeStruct((nb, t, hd), F32),
                 jax.ShapeDtypeStruct((nb, n_heads, D_HEAD, D_HEAD), F32))
    out_specs = (pl.BlockSpec((1, t, gw), lambda b, hg, ti: (b, 0, hg)),
                 pl.BlockSpec((1, g, D_HEAD, D_HEAD), lambda b, hg, ti: (b, hg, 0, 0)))
    return pl.pallas_call(
        functools.partial(_gdn_body, tb=tb, tv=t, sample=True, n_tb=1),
        out_shape=out_shape,
        grid=(nb, ng, 1),
        in_specs=in_specs,
        out_specs=out_specs,
        scratch_shapes=_gdn_scratch(tb),
        compiler_params=_cparams(("parallel", "parallel", "arbitrary")),
        name="gdn_sample",
    )(x, x, x, x, conv_prev, conv_prev, conv_prev, ab, w_conv, w_conv, w_conv,
      par, gn.reshape(1, D_HEAD), rec_prev)


def _layer(l, x, h, cache_sb_k, cache_sb_v, state_gdn_conv, state_gdn_rec, cache_mem_k,
           cache_mem_v, page_table, mem_prompt, w_in, w_conv, a_log, dt_bias, gdn_norm,
           sb_norm, sb_bias, w_out, norm_mem, mem_norm, w_mem_q, w_mem_k, w_mem_v, w_mem_o,
           norm_mlp, w_up, w_down, *, bp, tp, bs, ts):
    m, d_model = x.shape
    mp, ms = bp * tp, bs * ts
    n_heads = w_conv.shape[-1] // (3 * D_HEAD)
    hd = n_heads * D_HEAD
    n_sb = sb_bias.shape[-1]
    sbd = n_sb * D_HEAD
    gh = GDN_HEADS_PER_STEP
    ng = n_heads // gh
    tm = MM_TM if m % MM_TM == 0 else ROW_TILE

    w_in_t = jnp.swapaxes(w_in, 1, 2)[l]
    ab_col = 4 * hd
    sb_col = ab_col + 2 * n_heads
    tn_in = MM_TN
    while (4 * hd) % tn_in or (3 * sbd) % tn_in:
        tn_in //= 2
    proj = _matmul_nt(h, w_in_t, [(0, 4 * hd), (sb_col, 3 * sbd)], tm=tm, tn=tn_in, name="in_proj")
    w_a = w_in_t[ab_col:ab_col + n_heads].reshape(ng, gh, d_model)
    w_b = w_in_t[ab_col + n_heads:sb_col].reshape(ng, gh, d_model)
    w_ab = jnp.concatenate([w_a, w_b, jnp.zeros((ng, D_HEAD - 2 * gh, d_model), w_in_t.dtype)],
                           axis=1).reshape(ng * D_HEAD, d_model)
    gates = _matmul_nt(h, w_ab, [(0, ng * D_HEAD)], tm=tm, tn=D_HEAD, name="in_proj_gates")
    sb0 = 4 * n_heads

    par = _gdn_params(a_log[l], dt_bias[l], n_heads)
    mix, rec_p = _gdn_prompt(proj, gates, 0, w_conv[l], par, gdn_norm[l],
                             nb=bp, t=tp, n_heads=n_heads)
    samp = proj[mp:]
    xs_rows = samp[:, :4 * hd].reshape(bs, ts, 4 * hd)
    ab_s = gates[mp:].reshape(bs, ts, ng * D_HEAD)
    og_s, rec_s = _gdn_sample(xs_rows, state_gdn_conv[l], ab_s, state_gdn_rec[l], w_conv[l], par,
                              gdn_norm[l], n_heads=n_heads)
    conv_p = jnp.stack([lax.slice(proj, ((b + 1) * tp - (CONV_W - 1), 0), ((b + 1) * tp, 3 * hd))
                        for b in range(bp)])
    conv_s = jnp.concatenate([state_gdn_conv[l], xs_rows[:, :, :3 * hd]], axis=1)[:, -(CONV_W - 1):]

    mix = _sb_prompt(proj, mix, sb_bias[l].astype(F32), sb_norm[l], nb=bp, t=tp, n_heads=n_sb,
                     q_col=sb0, k_col=sb0 + n_sb, v_col=sb0 + 2 * n_sb, o_col=n_heads)
    q_s = samp[:, 4 * hd:4 * hd + sbd].reshape(bs, ts, sbd)
    k_s = samp[:, 4 * hd + sbd:4 * hd + 2 * sbd].reshape(bs, ts, sbd)
    v_s = samp[:, 4 * hd + 2 * sbd:].reshape(bs, ts, sbd)
    n_phys = cache_sb_k.shape[1]
    ck = cache_sb_k[l].reshape(n_phys, PAGE_SIZE * n_sb, D_HEAD)
    cv = cache_sb_v[l].reshape(n_phys, PAGE_SIZE * n_sb, D_HEAD)
    os_s = _sb_sample(q_s, k_s, v_s, ck, cv, page_table, sb_bias[l], sb_norm[l])
    mix_s = jnp.concatenate([og_s.reshape(ms, hd), os_s.reshape(ms, sbd)], axis=1).astype(BF16)
    mix = lax.dynamic_update_slice(mix, mix_s, (mp, 0))
    k_p = proj[:mp, 4 * hd + sbd:4 * hd + 2 * sbd].reshape(bp, tp, n_sb, D_HEAD)
    v_p = proj[:mp, 4 * hd + 2 * sbd:4 * hd + 3 * sbd].reshape(bp, tp, n_sb, D_HEAD)

    x = _matmul(mix, w_out[l], d_model, tm=tm, res=x, name="out_proj")

    n_mem = mem_prompt.shape[1]
    md = w_mem_k.shape[-1]
    memn = _rmsnorm(mem_prompt.reshape(bp * n_mem, d_model), mem_norm[l], BF16, n_mem)
    mk = _matmul(memn, w_mem_k[l], md, tm=bp * n_mem, name="mem_k_proj")
    mv = _matmul(memn, w_mem_v[l], md, tm=bp * n_mem, name="mem_v_proj")
    h = _rmsnorm(x, norm_mem[l], BF16, ROW_TILE)
    qm = _matmul(h, w_mem_q[l], md, tm=tm, out_dtype=BF16, name="mem_q_proj")
    om_p = _mem_attn(qm[:mp].reshape(bp, tp, md), mk.reshape(bp, n_mem, md), mv.reshape(bp, n_mem, md),
                     min(tp, 512), BF16)
    qm_s = jnp.pad(qm[mp:].astype(F32).reshape(bs, ts, md), ((0, 0), (0, SUBLANES - ts), (0, 0)))
    om_s = _mem_attn(qm_s, cache_mem_k[l].reshape(bs, n_mem, md), cache_mem_v[l].reshape(bs, n_mem, md),
                     SUBLANES, F32)[:, :ts]
    om = jnp.concatenate([om_p.reshape(mp, md), om_s.reshape(ms, md).astype(BF16)], axis=0)
    x = _matmul(om, w_mem_o[l], d_model, tm=tm, res=x, name="mem_o_proj")

    h = _rmsnorm(x, norm_mlp[l], BF16, ROW_TILE)
    d_ff = w_up.shape[-1]
    hid = _matmul(h, w_up[l], d_ff, tm=tm, relu2=True, out_dtype=BF16, name="mlp_up")
    half = d_ff // 2
    x = _matmul(hid, w_down[l], d_model, tm=tm, k=half, tn=MM_TN // 2, res=x, name="mlp_down_lo")
    x = _matmul(hid, w_down[l], d_model, tm=tm, k=half, tn=MM_TN // 2, a_col_block=1, w_row_block=1,
                res=x, name="mlp_down_hi")

    n_mh = md // D_HEAD
    outs = dict(k_p=k_p, v_p=v_p, k_s=k_s.reshape(bs, ts, n_sb, D_HEAD),
                v_s=v_s.reshape(bs, ts, n_sb, D_HEAD), conv_p=conv_p, conv_s=conv_s,
                rec_p=rec_p, rec_s=rec_s,
                mk=mk.reshape(bp, n_mem, n_mh, D_HEAD), mv=mv.reshape(bp, n_mem, n_mh, D_HEAD))
    return x, outs


def kernel(x_prompt, x_sample, cache_sb_k, cache_sb_v, state_gdn_conv, state_gdn_rec, cache_mem_k,
           cache_mem_v, page_table, mem_prompt, norm_mix, w_in, w_conv, a_log, dt_bias, gdn_norm,
           sb_norm, sb_bias, w_out, norm_mem, mem_norm, w_mem_q, w_mem_k, w_mem_v, w_mem_o,
           norm_mlp, w_up, w_down, norm_final):
    bp, tp, d_model = x_prompt.shape
    bs, ts, _ = x_sample.shape
    mp, ms = bp * tp, bs * ts
    depth = w_in.shape[0]
    h, x = _rmsnorm_join(x_prompt.reshape(mp, d_model), x_sample.reshape(ms, d_model), norm_mix[0])
    per_layer = []
    for l in range(depth):
        if l > 0:
            h = _rmsnorm(x, norm_mix[l], BF16, ROW_TILE)
        x, outs = _layer(
            l, x, h, cache_sb_k, cache_sb_v, state_gdn_conv, state_gdn_rec, cache_mem_k,
            cache_mem_v, page_table, mem_prompt, w_in, w_conv, a_log, dt_bias, gdn_norm,
            sb_norm, sb_bias, w_out, norm_mem, mem_norm, w_mem_q, w_mem_k, w_mem_v, w_mem_o,
            norm_mlp, w_up, w_down, bp=bp, tp=tp, bs=bs, ts=ts)
        per_layer.append(outs)
    y_p = _rmsnorm(x, norm_final, F32, ROW_TILE, rows=mp).reshape(bp, tp, d_model)
    y_s = _rmsnorm(x, norm_final, F32, ms, row_block_offset=mp // ms, rows=ms).reshape(bs, ts, d_model)

    def stack(name):
        return jnp.stack([o[name] for o in per_layer])

    return (y_p, y_s, stack("k_p"), stack("v_p"), stack("k_s"), stack("v_s"),
            stack("conv_p"), stack("conv_s"), stack("rec_p"), stack("rec_s"),
            stack("mk"), stack("mv"))
```

```python
import functools
import math

import jax
import jax.numpy as jnp
from jax import lax
from jax.experimental import pallas as pl
from jax.experimental.pallas import tpu as pltpu

F32 = jnp.float32
BF16 = jnp.bfloat16

EPS = 1e-6
D_HEAD = 128
SUBLANES = 8
GDN_CHUNK = 64
GDN_HEADS_PER_STEP = 8
GDN_PROMPT_ROWS = 256
CONV_W = 4
PAGE_SIZE = 128
SB_TQ = 512
SB_TK = 128
SB_PAGES_PER_STEP = 4
V7X_VMEM_LIMIT_BYTES = 56 * 1024 * 1024
MM_TN = 512
MM_TM = 640
ROW_TILE = 128


def _cparams(sem):
    return pltpu.CompilerParams(dimension_semantics=sem, vmem_limit_bytes=V7X_VMEM_LIMIT_BYTES)


def _softplus(x):
    return jnp.maximum(x, 0.0) + jnp.log1p(jnp.exp(-jnp.abs(x)))


def _softplus_plain_log(x):
    return jnp.maximum(x, 0.0) + jnp.log(1.0 + jnp.exp(-jnp.abs(x)))


def _sigmoid(x):
    return 1.0 / (1.0 + jnp.exp(-x))


def _dot(a, b):
    return jnp.dot(a, b, preferred_element_type=F32)


def _dot_nt(a, b):
    return lax.dot_general(a, b, (((1,), (1,)), ((), ())), preferred_element_type=F32)


def _split2(x):
    hi = x.astype(BF16)
    lo = (x - hi.astype(F32)).astype(BF16)
    return hi, lo


def _split3(x):
    hi = x.astype(BF16)
    r = x - hi.astype(F32)
    mid = r.astype(BF16)
    lo = (r - mid.astype(F32)).astype(BF16)
    return hi, mid, lo


def _dot_hp(a, b):
    a_hi, a_lo = _split2(a)
    b_hi, b_lo = _split2(b)
    return _dot(a_hi, b_hi) + (_dot(a_hi, b_lo) + _dot(a_lo, b_hi))


def _rms(x, g):
    y = x * lax.rsqrt(jnp.mean(x * x, axis=-1, keepdims=True) + EPS)
    return y * g


def _rmsnorm_body(x_ref, g_ref, o_ref):
    o_ref[...] = _rms(x_ref[...], g_ref[...]).astype(o_ref.dtype)


def _rmsnorm(x, g, out_dtype, tm, row_block_offset=0, rows=None):
    m, d = x.shape
    rows = m if rows is None else rows
    off = row_block_offset
    assert rows % tm == 0
    return pl.pallas_call(
        _rmsnorm_body,
        out_shape=jax.ShapeDtypeStruct((rows, d), out_dtype),
        grid=(rows // tm,),
        in_specs=[pl.BlockSpec((tm, d), lambda i: (i + off, 0)),
                  pl.BlockSpec((1, d), lambda i: (0, 0))],
        out_specs=pl.BlockSpec((tm, d), lambda i: (i, 0)),
        compiler_params=_cparams(("parallel",)),
        name="rmsnorm",
    )(x, g.reshape(1, d))


def _rmsnorm_join_body(xp_ref, xs_ref, g_ref, h_ref, x_ref, *, n_prompt_blocks):
    i = pl.program_id(0)

    @pl.when(i < n_prompt_blocks)
    def _():
        x = xp_ref[...]
        x_ref[...] = x
        h_ref[...] = _rms(x, g_ref[...]).astype(h_ref.dtype)

    @pl.when(i >= n_prompt_blocks)
    def _():
        x = xs_ref[...]
        x_ref[...] = x
        h_ref[...] = _rms(x, g_ref[...]).astype(h_ref.dtype)


def _rmsnorm_join(xp, xs, g):
    mp, d = xp.shape
    ms = xs.shape[0]
    tm = ms
    assert mp % tm == 0
    npb = mp // tm
    return pl.pallas_call(
        functools.partial(_rmsnorm_join_body, n_prompt_blocks=npb),
        out_shape=(jax.ShapeDtypeStruct((mp + ms, d), BF16),
                   jax.ShapeDtypeStruct((mp + ms, d), F32)),
        grid=(npb + 1,),
        in_specs=[pl.BlockSpec((tm, d), lambda i: (jnp.minimum(i, npb - 1), 0)),
                  pl.BlockSpec((tm, d), lambda i: (0, 0)),
                  pl.BlockSpec((1, d), lambda i: (0, 0))],
        out_specs=(pl.BlockSpec((tm, d), lambda i: (i, 0)),
                   pl.BlockSpec((tm, d), lambda i: (i, 0))),
        compiler_params=_cparams(("arbitrary",)),
        name="rmsnorm_join",
    )(xp, xs, g.reshape(1, d))


def _matmul_body(*refs, relu2, has_res):
    if has_res:
        a_ref, w_ref, r_ref, o_ref, wbf_ref = refs
    else:
        a_ref, w_ref, o_ref, wbf_ref = refs

    @pl.when(pl.program_id(1) == 0)
    def _():
        wbf_ref[...] = w_ref[...].astype(BF16)

    acc = _dot(a_ref[...], wbf_ref[...])
    if relu2:
        acc = jnp.square(jnp.maximum(acc, 0.0))
    if has_res:
        acc = acc + r_ref[...]
    o_ref[...] = acc.astype(o_ref.dtype)


def _matmul(a, w, n, *, tm, k=None, tn=MM_TN, a_col_block=0, w_row_block=0, w_col_block=0,
            res=None, relu2=False, out_dtype=F32, w_single_buffer=False, name="matmul"):
    m = a.shape[0]
    k = a.shape[1] if k is None else k
    while n % tn:
        tn //= 2
    assert m % tm == 0 and tn % D_HEAD == 0
    w_mode = dict(pipeline_mode=pl.Buffered(1)) if w_single_buffer else {}
    in_specs = [pl.BlockSpec((tm, k), lambda j, i: (i, a_col_block)),
                pl.BlockSpec((k, tn), lambda j, i: (w_row_block, j + w_col_block), **w_mode)]
    args = [a, w]
    if res is not None:
        in_specs.append(pl.BlockSpec((tm, tn), lambda j, i: (i, j)))
        args.append(res)
    return pl.pallas_call(
        functools.partial(_matmul_body, relu2=relu2, has_res=res is not None),
        out_shape=jax.ShapeDtypeStruct((m, n), out_dtype),
        grid=(n // tn, m // tm),
        in_specs=in_specs,
        out_specs=pl.BlockSpec((tm, tn), lambda j, i: (i, j)),
        scratch_shapes=[pltpu.VMEM((k, tn), BF16)],
        compiler_params=_cparams(("parallel", "arbitrary")),
        name=name,
    )(*args)


def _matmul_nt_body(a_ref, w_ref, o_ref, wbf_ref):
    @pl.when(pl.program_id(1) == 0)
    def _():
        wbf_ref[...] = w_ref[...].astype(BF16)

    o_ref[...] = _dot_nt(a_ref[...], wbf_ref[...]).astype(o_ref.dtype)


def _matmul_nt(a, wt, row_segments, *, tm, tn=MM_TN, name="matmul_nt"):
    m, k = a.shape
    starts = []
    for first, n_rows in row_segments:
        assert first % SUBLANES == 0 and n_rows % tn == 0
        starts += [first + i * tn for i in range(n_rows // tn)]
    n = tn * len(starts)
    def w_row(j):
        row8 = j * (tn // SUBLANES)
        base, shift = 0, 0
        for first, n_rows in row_segments:
            step = (first - base * tn) - shift
            if step:
                row8 = row8 + jnp.where(j >= base, step // SUBLANES, 0)
            shift += step
            base += n_rows // tn
        return row8 * SUBLANES

    return pl.pallas_call(
        _matmul_nt_body,
        out_shape=jax.ShapeDtypeStruct((m, n), F32),
        grid=(n // tn, m // tm),
        in_specs=[pl.BlockSpec((tm, k), lambda j, i: (i, 0)),
                  pl.BlockSpec((pl.Element(tn), pl.Element(k)), lambda j, i: (w_row(j), 0))],
        out_specs=pl.BlockSpec((tm, tn), lambda j, i: (i, j)),
        scratch_shapes=[pltpu.VMEM((tn, k), BF16)],
        compiler_params=_cparams(("parallel", "arbitrary")),
        name=name,
    )(a, wt)


def _mem_attn_body(q_ref, k_ref, v_ref, o_ref, *, n_heads, scale):
    for h in range(n_heads):
        sl = slice(h * D_HEAD, (h + 1) * D_HEAD)
        q = q_ref[0, :, sl].astype(BF16)
        k = k_ref[0, :, sl].astype(BF16)
        v = v_ref[0, :, sl].astype(BF16)
        s = _dot_nt(q, k) * scale
        s = s - jnp.max(s, axis=-1, keepdims=True)
        e = jnp.exp(s)
        p = e / jnp.sum(e, axis=-1, keepdims=True)
        o_ref[0, :, sl] = _dot(p.astype(BF16), v).astype(o_ref.dtype)


def _mem_attn(q, k, v, tq, out_dtype):
    nb, t, hd = q.shape
    n_mem = k.shape[1]
    return pl.pallas_call(
        functools.partial(_mem_attn_body, n_heads=hd // D_HEAD, scale=D_HEAD ** -0.5),
        out_shape=jax.ShapeDtypeStruct((nb, t, hd), out_dtype),
        grid=(nb, t // tq),
        in_specs=[pl.BlockSpec((1, tq, hd), lambda b, i: (b, i, 0)),
                  pl.BlockSpec((1, n_mem, hd), lambda b, i: (b, 0, 0)),
                  pl.BlockSpec((1, n_mem, hd), lambda b, i: (b, 0, 0))],
        out_specs=pl.BlockSpec((1, tq, hd), lambda b, i: (b, i, 0)),
        compiler_params=_cparams(("parallel", "parallel")),
        name="mem_attn",
    )(q, k, v)


def _strict_upper_ones(n):
    r = lax.broadcasted_iota(jnp.int32, (n, n), 0)
    c = lax.broadcasted_iota(jnp.int32, (n, n), 1)
    return jnp.where(r > c, 1.0, 0.0).astype(BF16)


def _sb_blocks(zs, valids, c, u):
    sps = [_softplus_plain_log(z) for z in zs]
    lss = [-sp if v is None else jnp.where(v, -sp, 0.0) for sp, v in zip(sps, valids)]
    parts = [_split2(ls) for ls in lss]
    afters = [_dot(hi, u) + _dot(lo, u) for hi, lo in parts]
    out = []
    for z, sp, ls, after, v in zip(zs, sps, lss, afters, valids):
        a = jnp.exp((z - sp) + after + c)
        out.append(a if v is None else jnp.where(v, a, 0.0))
        c = c + jnp.sum(ls, axis=-1, keepdims=True)
    return out, c


def _sb_prompt_body(bias_ref, q_ref, k_ref, v_ref, g_ref, mix_ref, o_ref, *, tq, tk, scale):
    del mix_ref
    h = pl.program_id(1)
    qi = pl.program_id(2)
    r = tq // tk
    bias = bias_ref[h]
    q = q_ref[...].astype(BF16)
    u = _strict_upper_ones(tk)

    def blocks(qrows, row0, j_hi, masks, c, acc):
        n = qrows.shape[0]
        k0s = [pl.multiple_of((j_hi - i) * tk, tk) for i in range(len(masks))]
        zs = [_dot_nt(qrows, k_ref[pl.ds(k0, tk), :].astype(BF16)) * scale + bias for k0 in k0s]
        row = lax.broadcasted_iota(jnp.int32, (n, tk), 0) + (qi * tq + row0)
        col = lax.broadcasted_iota(jnp.int32, (n, tk), 1)
        valids = [((k0 + col) < row) if m else None for k0, m in zip(k0s, masks)]
        ws, c = _sb_blocks(zs, valids, c, u)
        for w, k0 in zip(ws, k0s):
            acc = acc + _dot(w.astype(BF16), v_ref[pl.ds(k0, tk), :].astype(BF16))
        return c, acc

    hr = r // 2
    half = tq // 2
    zc = jnp.zeros((half, 1), F32)
    za = jnp.zeros((half, D_HEAD), F32)
    c_top, acc_top = blocks(q[:half], 0, qi * r + hr - 1, [True] * hr, zc, za)
    c_bot, acc_bot = blocks(q[half:], half, qi * r + r - 1, [True] * hr + [False] * hr, zc, za)
    c = jnp.concatenate([c_top, c_bot], axis=0)
    acc = jnp.concatenate([acc_top, acc_bot], axis=0)

    def body(it, carry):
        return blocks(q, 0, (qi - 1 - it) * r + (r - 1), [False] * r, *carry)

    c, acc = lax.fori_loop(0, qi, body, (c, acc))
    o_ref[...] = _rms(acc, g_ref[...]).astype(o_ref.dtype)


def _sb_prompt(qkv, mix, bias, gain, *, nb, t, n_heads, q_col, k_col, v_col, o_col):
    tq, tk = SB_TQ, SB_TK
    nq = t // tq
    return pl.pallas_call(
        functools.partial(_sb_prompt_body, tq=tq, tk=tk, scale=D_HEAD ** -0.5),
        out_shape=jax.ShapeDtypeStruct(mix.shape, mix.dtype),
        grid=(nb, n_heads, nq),
        in_specs=[pl.BlockSpec(memory_space=pltpu.SMEM),
                  pl.BlockSpec((tq, D_HEAD), lambda b, h, i: (b * nq + i, q_col + h)),
                  pl.BlockSpec((t, D_HEAD), lambda b, h, i: (b, k_col + h)),
                  pl.BlockSpec((t, D_HEAD), lambda b, h, i: (b, v_col + h)),
                  pl.BlockSpec((1, D_HEAD), lambda b, h, i: (0, 0)),
                  pl.BlockSpec(memory_space=pl.ANY)],
        out_specs=pl.BlockSpec((tq, D_HEAD), lambda b, h, i: (b * nq + i, o_col + h)),
        input_output_aliases={5: 0},
        compiler_params=_cparams(("parallel", "parallel", "arbitrary")),
        name="sb_prompt",
    )(bias, qkv, qkv, qkv, gain.reshape(1, D_HEAD), mix)


def _sb_sample_body(pt_ref, q_ref, kn_ref, vn_ref, *rest, t, n_heads, n_steps, pages_per_step, scale):
    del pt_ref
    pps = pages_per_step
    ck_refs = rest[:pps]
    cv_refs = rest[pps:2 * pps]
    bias_ref, g_ref, o_ref, qall_ref, c_ref, acc_ref, new_ref, kv16_ref = rest[2 * pps:]
    s = pl.program_id(1)
    n_pairs = n_heads // 2
    rp = 2 * SUBLANES
    rows = n_pairs * rp
    tk = PAGE_SIZE
    cols = 2 * tk
    rr = lax.broadcasted_iota(jnp.int32, (cols, cols), 0)
    cc = lax.broadcasted_iota(jnp.int32, (cols, cols), 1)
    u2 = jnp.where(jnp.logical_and((rr >> 1) > (cc >> 1), (rr & 1) == (cc & 1)), 1.0, 0.0).astype(BF16)
    qrow = lax.broadcasted_iota(jnp.int32, (rows, cols), 0)
    col = lax.broadcasted_iota(jnp.int32, (rows, cols), 1)
    own = ((qrow >> 3) & 1) == (col & 1)

    def process(get_ks, get_vs, valids):
        qb = [qall_ref[g * rp:(g + 1) * rp, :].astype(BF16) for g in range(n_pairs)]
        zs = [jnp.concatenate([_dot_nt(qb[g], get_k(g)) for g in range(n_pairs)], axis=0) * scale
              + bias_ref[...] for get_k in get_ks]
        ws, c_new = _sb_blocks(zs, valids, c_ref[...], u2)
        ws = [w.astype(BF16) for w in ws]
        outs = []
        for g in range(n_pairs):
            sl = slice(g * rp, (g + 1) * rp)
            o = acc_ref[sl, :]
            for w, get_v in zip(ws, get_vs):
                o = o + _dot(w[sl, :], get_v(g))
            outs.append(o)
        c_ref[...] = c_new
        acc_ref[...] = jnp.concatenate(outs, axis=0)

    @pl.when(s == 0)
    def _():
        qall_ref[...] = jnp.zeros(qall_ref.shape, F32)
        for hd in range(n_heads):
            qall_ref[hd * SUBLANES:hd * SUBLANES + t, :] = q_ref[0, :, hd * D_HEAD:(hd + 1) * D_HEAD]
        c_ref[...] = jnp.zeros(c_ref.shape, F32)
        acc_ref[...] = jnp.zeros(acc_ref.shape, F32)
        new_ref[...] = jnp.zeros(new_ref.shape, F32)
        for hd in range(n_heads):
            for j in range(t):
                r = 2 * j + (hd & 1)
                new_ref[0, hd >> 1, r:r + 1, :] = kn_ref[0, j:j + 1, hd * D_HEAD:(hd + 1) * D_HEAD]
                new_ref[1, hd >> 1, r:r + 1, :] = vn_ref[0, j:j + 1, hd * D_HEAD:(hd + 1) * D_HEAD]
        qi = qrow & (SUBLANES - 1)
        valid = jnp.logical_and(own, jnp.logical_and((col >> 1) < qi, qi < t))
        process([lambda g: new_ref[0, g].astype(BF16)], [lambda g: new_ref[1, g].astype(BF16)], [valid])

    @pl.when(s > 0)
    def _():
        def pair(slot, ref):
            kv16_ref[slot] = pltpu.bitcast(ref[0].astype(BF16), jnp.uint32)

            def get(g):
                return pltpu.bitcast(kv16_ref[slot, pl.ds(g, tk, stride=n_pairs), :], BF16)
            return get

        process([pair(p, r) for p, r in enumerate(ck_refs)],
                [pair(pps + p, r) for p, r in enumerate(cv_refs)], [own] * pps)

    @pl.when(s == n_steps - 1)
    def _():
        for hd in range(n_heads):
            o = acc_ref[hd * SUBLANES:hd * SUBLANES + t, :]
            o_ref[0, :, hd * D_HEAD:(hd + 1) * D_HEAD] = _rms(o, g_ref[...])


def _sb_sample(q, k_new, v_new, cache_k, cache_v, page_table, bias, gain):
    nb, t, hd = q.shape
    n_heads = hd // D_HEAD
    n_pages = page_table.shape[1]
    assert t <= SUBLANES and n_heads % 2 == 0
    rows = n_heads * SUBLANES
    bias_rows = jnp.repeat(bias.astype(F32), SUBLANES).reshape(rows, 1)

    pps = SB_PAGES_PER_STEP
    while n_pages % pps:
        pps //= 2
    n_steps = n_pages // pps + 1

    def page_spec(p):
        def idx(b, s, pt):
            return (pt[b, n_pages - 1 - ((jnp.maximum(s, 1) - 1) * pps + p)], 0, 0)
        return pl.BlockSpec((1, PAGE_SIZE * n_heads, D_HEAD), idx)

    tok_spec = pl.BlockSpec((1, t, hd), lambda b, s, pt: (b, 0, 0))
    grid_spec = pltpu.PrefetchScalarGridSpec(
        num_scalar_prefetch=1,
        grid=(nb, n_steps),
        in_specs=[tok_spec, tok_spec, tok_spec]
                 + [page_spec(p) for p in range(pps)] + [page_spec(p) for p in range(pps)]
                 + [pl.BlockSpec((rows, 1), lambda b, s, pt: (0, 0)),
                    pl.BlockSpec((1, D_HEAD), lambda b, s, pt: (0, 0))],
        out_specs=pl.BlockSpec((1, t, hd), lambda b, s, pt: (b, 0, 0)),
        scratch_shapes=[pltpu.VMEM((rows, D_HEAD), F32),
                        pltpu.VMEM((rows, 1), F32),
                        pltpu.VMEM((rows, D_HEAD), F32),
                        pltpu.VMEM((2, n_heads // 2, 2 * PAGE_SIZE, D_HEAD), F32),
                        pltpu.VMEM((2 * pps, PAGE_SIZE * n_heads // 2, D_HEAD), jnp.uint32)],
    )
    return pl.pallas_call(
        functools.partial(_sb_sample_body, t=t, n_heads=n_heads, n_steps=n_steps,
                          pages_per_step=pps, scale=D_HEAD ** -0.5),
        out_shape=jax.ShapeDtypeStruct((nb, t, hd), F32),
        grid_spec=grid_spec,
        compiler_params=_cparams(("parallel", "arbitrary")),
        name="sb_sample",
    )(page_table, q, k_new, v_new, *([cache_k] * pps), *([cache_v] * pps), bias_rows,
      gain.reshape(1, D_HEAD))


def _unit_lower_inverses(ms, eye):
    ps = [-m for m in ms]
    ts = [eye + p for p in ps]
    for _ in range(int(math.log2(ms[0].shape[0])) - 1):
        ps = [_dot_hp(p, p) for p in ps]
        ts = [t + _dot_hp(t, p) for t, p in zip(ts, ps)]
    return ts


def _gdn_body(*refs, tb, tv, sample, n_tb):
    g_heads = GDN_HEADS_PER_STEP
    gw = g_heads * D_HEAD
    c_len = GDN_CHUNK
    if sample:
        (xq_ref, xk_ref, xv_ref, z_ref, pq_ref, pk_ref, pv_ref, ab_ref, wq_ref, wk_ref, wv_ref,
         par_ref, gn_ref, s0_ref, o_ref, rec_ref, ext_ref, qkv_ref, gate_ref, s_ref) = refs
    else:
        (xq_ref, xk_ref, xv_ref, z_ref, pq_ref, pk_ref, pv_ref, ab_ref, wq_ref, wk_ref, wv_ref,
         par_ref, gn_ref, o_ref, rec_ref, ext_ref, qkv_ref, gate_ref, s_ref) = refs
    ti = pl.program_id(2)

    @pl.when(ti == 0)
    def _():
        if sample:
            s_ref[...] = s0_ref[0]
        else:
            s_ref[...] = jnp.zeros(s_ref.shape, F32)

    for part, (x_ref, p_ref, w_ref) in enumerate(((xq_ref, pq_ref, wq_ref),
                                                   (xk_ref, pk_ref, wk_ref),
                                                   (xv_ref, pv_ref, wv_ref))):
        if sample:
            ext_ref[part, 0:SUBLANES, :] = jnp.zeros((SUBLANES, gw), F32)
            ext_ref[part, SUBLANES - (CONV_W - 1):SUBLANES, :] = p_ref[0]
            ext_ref[part, SUBLANES:SUBLANES + tb, :] = jnp.zeros((tb, gw), F32)
            ext_ref[part, SUBLANES:SUBLANES + tv, :] = x_ref[0]
        else:
            ext_ref[part, 0:SUBLANES, :] = jnp.where(ti > 0, p_ref[...], 0.0)
            ext_ref[part, SUBLANES:SUBLANES + tb, :] = x_ref[...]
        w = w_ref[...]
        base = SUBLANES - (CONV_W - 1)
        y = ext_ref[part, base:base + tb, :] * w[0:1, :]
        for i in range(1, CONV_W):
            y = y + ext_ref[part, base + i:base + i + tb, :] * w[i:i + 1, :]
        y = y * _sigmoid(y)
        if part < 2:
            for g in range(g_heads):
                sl = slice(g * D_HEAD, (g + 1) * D_HEAD)
                seg = y[:, sl]
                seg = seg * lax.rsqrt(jnp.sum(seg * seg, axis=-1, keepdims=True) + EPS)
                if part == 0:
                    seg = seg * (D_HEAD ** -0.5)
                qkv_ref[part, :, sl] = seg
        else:
            qkv_ref[part] = y

    if sample:
        gate_ref[0] = jnp.zeros((tb, D_HEAD), F32)
        gate_ref[0, 0:tv, :] = ab_ref[0]
        ab = gate_ref[0]
    else:
        ab = ab_ref[...]
    a_log = par_ref[0, 0:1, :]
    dt_bias = par_ref[0, 1:2, :]
    rowi = lax.broadcasted_iota(jnp.int32, (tb, D_HEAD), 0)
    lane = lax.broadcasted_iota(jnp.int32, (tb, D_HEAD), 1)
    live = rowi < tv
    gdec = -jnp.exp(a_log) * _softplus(ab + dt_bias)
    gate_ref[0] = jnp.where(jnp.logical_and(live, lane < g_heads), gdec, 0.0)
    gate_ref[1] = jnp.where(live, _sigmoid(ab), 0.0)

    ri = lax.broadcasted_iota(jnp.int32, (c_len, c_len), 0)
    ci = lax.broadcasted_iota(jnp.int32, (c_len, c_len), 1)
    incl = ri >= ci
    strict = ri > ci
    eye = jnp.where(ri == ci, 1.0, 0.0).astype(F32)
    tri = jnp.where(ri <= ci, 1.0, 0.0).astype(BF16)
    gn = gn_ref[...]

    def chunk(ck, carry):
        r0 = pl.multiple_of(ck * c_len, c_len)
        rs = pl.ds(r0, c_len)
        g_t = gate_ref[0, rs, :].T
        hi, mid, lo = _split3(g_t)
        gc_t = _dot(hi, tri) + (_dot(mid, tri) + _dot(lo, tri))
        gc = gc_t.T
        beta_all = gate_ref[1, rs, :]
        hs = range(g_heads)
        sls = [slice(g * D_HEAD, (g + 1) * D_HEAD) for g in hs]
        q = [qkv_ref[0, rs, sl] for sl in sls]
        k = [qkv_ref[1, rs, sl] for sl in sls]
        v = [qkv_ref[2, rs, sl] for sl in sls]
        s_old = [s_ref[g] for g in hs]
        if sample:
            zz = [z_ref[0, :, sl] for sl in sls]
        else:
            zz = [z_ref[rs, sl] for sl in sls]
        gcol = [gc[:, g:g + 1] for g in hs]
        grow = [gc_t[g:g + 1, :] for g in hs]
        glast = [gcol[g][c_len - 1:c_len, :] for g in hs]
        beta = [beta_all[:, g_heads + g:g_heads + g + 1] for g in hs]
        decay = [jnp.exp(jnp.where(incl, gcol[g] - grow[g], -jnp.inf)) for g in hs]
        eg = [jnp.exp(gcol[g]) for g in hs]
        kb = [k[g] * beta[g] for g in hs]
        vb = [v[g] * beta[g] for g in hs]
        k16 = [k[g].astype(BF16) for g in hs]
        m = [jnp.where(strict, _dot_nt(kb[g].astype(BF16), k16[g]) * decay[g], 0.0) for g in hs]
        qk = [jnp.where(incl, _dot_nt(q[g].astype(BF16), k16[g]) * decay[g], 0.0) for g in hs]
        tmat = _unit_lower_inverses(m, eye)
        rhs = [jnp.concatenate([vb[g], kb[g] * eg[g]], axis=1).astype(BF16) for g in hs]
        uw = [_dot(tmat[g].astype(BF16), rhs[g]) for g in hs]
        s16 = [s_old[g].astype(BF16) for g in hs]
        ws = [_dot(jnp.concatenate([uw[g][:, D_HEAD:], q[g] * eg[g]], axis=0).astype(BF16), s16[g])
              for g in hs]
        v_new = [uw[g][:, :D_HEAD] - ws[g][:c_len] for g in hs]
        v16 = [v_new[g].astype(BF16) for g in hs]
        o = [ws[g][c_len:] + _dot(qk[g].astype(BF16), v16[g]) for g in hs]
        kg_t = [(k[g] * jnp.exp(glast[g] - gcol[g])).T.astype(BF16) for g in hs]
        s_new = [s_old[g] * jnp.exp(glast[g]) + _dot(kg_t[g], v16[g]) for g in hs]
        out = [_rms(o[g], gn) for g in hs]
        if sample:
            out = [out[g][0:tv] * (zz[g] * _sigmoid(zz[g])) for g in hs]
            o_ref[0] = jnp.concatenate(out, axis=1)
        else:
            out = [out[g] * (zz[g] * _sigmoid(zz[g])) for g in hs]
            o_ref[rs, :] = jnp.concatenate(out, axis=1).astype(o_ref.dtype)
        for g in hs:
            s_ref[g] = s_new[g]
        return carry

    lax.fori_loop(0, tb // c_len, chunk, 0)

    @pl.when(ti == n_tb - 1)
    def _():
        rec_ref[0] = s_ref[...]


def _gdn_scratch(tb):
    g = GDN_HEADS_PER_STEP
    gw = g * D_HEAD
    return [pltpu.VMEM((3, tb + SUBLANES, gw), F32),
            pltpu.VMEM((3, tb, gw), F32),
            pltpu.VMEM((2, tb, D_HEAD), F32),
            pltpu.VMEM((g, D_HEAD, D_HEAD), F32)]


def _gdn_params(a_log, dt_bias, n_heads):
    g = GDN_HEADS_PER_STEP
    ng = n_heads // g
    par = jnp.zeros((ng, SUBLANES, D_HEAD), F32)
    par = par.at[:, 0, :g].set(a_log.reshape(ng, g).astype(F32))
    par = par.at[:, 1, :g].set(dt_bias.reshape(ng, g).astype(F32))
    return par


def _gdn_prompt(qkvz, gates, gate_col, w_conv, par, gn, *, nb, t, n_heads):
    g = GDN_HEADS_PER_STEP
    gw = g * D_HEAD
    ng = n_heads // g
    tb = GDN_PROMPT_ROWS
    n_tb = t // tb
    hd = n_heads * D_HEAD
    m = qkvz.shape[0]

    def cur(part):
        return pl.BlockSpec((tb, gw), lambda b, hg, ti: (b * n_tb + ti, part * ng + hg))

    def prev(part):
        return pl.BlockSpec(
            (SUBLANES, gw),
            lambda b, hg, ti: (jnp.maximum((b * n_tb + ti) * (tb // SUBLANES) - 1, 0), part * ng + hg))

    def wspec(part):
        return pl.BlockSpec((CONV_W, gw), lambda b, hg, ti: (0, part * ng + hg))

    in_specs = [cur(0), cur(1), cur(2), cur(3), prev(0), prev(1), prev(2),
                pl.BlockSpec((tb, D_HEAD), lambda b, hg, ti: (b * n_tb + ti, gate_col + hg)),
                wspec(0), wspec(1), wspec(2),
                pl.BlockSpec((1, SUBLANES, D_HEAD), lambda b, hg, ti: (hg, 0, 0)),
                pl.BlockSpec((1, D_HEAD), lambda b, hg, ti: (0, 0))]
    out_shape = (jax.ShapeDtypeStruct((m, 2 * hd), BF16),
                 jax.ShapeDtypeStruct((nb, n_heads, D_HEAD, D_HEAD), F32))
    out_specs = (pl.BlockSpec((tb, gw), lambda b, hg, ti: (b * n_tb + ti, hg)),
                 pl.BlockSpec((1, g, D_HEAD, D_HEAD), lambda b, hg, ti: (b, hg, 0, 0)))
    return pl.pallas_call(
        functools.partial(_gdn_body, tb=tb, tv=tb, sample=False, n_tb=n_tb),
        out_shape=out_shape,
        grid=(nb, ng, n_tb),
        in_specs=in_specs,
        out_specs=out_specs,
        scratch_shapes=_gdn_scratch(tb),
        compiler_params=_cparams(("parallel", "parallel", "arbitrary")),
        name="gdn_prompt",
    )(qkvz, qkvz, qkvz, qkvz, qkvz, qkvz, qkvz, gates, w_conv, w_conv, w_conv,
      par, gn.reshape(1, D_HEAD))


def _gdn_sample(x, conv_prev, ab, rec_prev, w_conv, par, gn, *, n_heads):
    nb, t, _ = x.shape
    g = GDN_HEADS_PER_STEP
    gw = g * D_HEAD
    ng = n_heads // g
    tb = GDN_CHUNK
    hd = n_heads * D_HEAD
    assert t <= tb

    def cur(part):
        return pl.BlockSpec((1, t, gw), lambda b, hg, ti: (b, 0, part * ng + hg))

    def prev(part):
        return pl.BlockSpec((1, CONV_W - 1, gw), lambda b, hg, ti: (b, 0, part * ng + hg))

    def wspec(part):
        return pl.BlockSpec((CONV_W, gw), lambda b, hg, ti: (0, part * ng + hg))

    in_specs = [cur(0), cur(1), cur(2), cur(3), prev(0), prev(1), prev(2),
                pl.BlockSpec((1, t, D_HEAD), lambda b, hg, ti: (b, 0, hg)),
                wspec(0), wspec(1), wspec(2),
                pl.BlockSpec((1, SUBLANES, D_HEAD), lambda b, hg, ti: (hg, 0, 0)),
                pl.BlockSpec((1, D_HEAD), lambda b, hg, ti: (0, 0)),
                pl.BlockSpec((1, g, D_HEAD, D_HEAD), lambda b, hg, ti: (b, hg, 0, 0))]
    out_shape = (jax.ShapeDtypeStruct((nb, t, hd), F32),
                 jax.ShapeDtypeStruct((nb, n_heads, D_HEAD, D_HEAD), F32))
    out_specs = (pl.BlockSpec((1, t, gw), lambda b, hg, ti: (b, 0, hg)),
                 pl.BlockSpec((1, g, D_HEAD, D_HEAD), lambda b, hg, ti: (b, hg, 0, 0)))
    return pl.pallas_call(
        functools.partial(_gdn_body, tb=tb, tv=t, sample=True, n_tb=1),
        out_shape=out_shape,
        grid=(nb, ng, 1),
        in_specs=in_specs,
        out_specs=out_specs,
        scratch_shapes=_gdn_scratch(tb),
        compiler_params=_cparams(("parallel", "parallel", "arbitrary")),
        name="gdn_sample",
    )(x, x, x, x, conv_prev, conv_prev, conv_prev, ab, w_conv, w_conv, w_conv,
      par, gn.reshape(1, D_HEAD), rec_prev)


def _layer(l, x, h, cache_sb_k, cache_sb_v, state_gdn_conv, state_gdn_rec, cache_mem_k,
           cache_mem_v, page_table, mem_prompt, w_in, w_conv, a_log, dt_bias, gdn_norm,
           sb_norm, sb_bias, w_out, norm_mem, mem_norm, w_mem_q, w_mem_k, w_mem_v, w_mem_o,
           norm_mlp, w_up, w_down, *, bp, tp, bs, ts):
    m, d_model = x.shape
    mp, ms = bp * tp, bs * ts
    n_heads = w_conv.shape[-1] // (3 * D_HEAD)
    hd = n_heads * D_HEAD
    n_sb = sb_bias.shape[-1]
    sbd = n_sb * D_HEAD
    gh = GDN_HEADS_PER_STEP
    ng = n_heads // gh
    tm = MM_TM if m % MM_TM == 0 else ROW_TILE

    w_in_t = jnp.swapaxes(w_in, 1, 2)[l]
    ab_col = 4 * hd
    sb_col = ab_col + 2 * n_heads
    tn_in = MM_TN
    while (4 * hd) % tn_in or (3 * sbd) % tn_in:
        tn_in //= 2
    proj = _matmul_nt(h, w_in_t, [(0, 4 * hd), (sb_col, 3 * sbd)], tm=tm, tn=tn_in, name="in_proj")
    w_a = w_in_t[ab_col:ab_col + n_heads].reshape(ng, gh, d_model)
    w_b = w_in_t[ab_col + n_heads:sb_col].reshape(ng, gh, d_model)
    w_ab = jnp.concatenate([w_a, w_b, jnp.zeros((ng, D_HEAD - 2 * gh, d_model), w_in_t.dtype)],
                           axis=1).reshape(ng * D_HEAD, d_model)
    gates = _matmul_nt(h, w_ab, [(0, ng * D_HEAD)], tm=tm, tn=ng * D_HEAD, name="in_proj_gates")
    sb0 = 4 * n_heads

    par = _gdn_params(a_log[l], dt_bias[l], n_heads)
    mix, rec_p = _gdn_prompt(proj, gates, 0, w_conv[l], par, gdn_norm[l],
                             nb=bp, t=tp, n_heads=n_heads)
    samp = proj[mp:]
    xs_rows = samp[:, :4 * hd].reshape(bs, ts, 4 * hd)
    ab_s = gates[mp:].reshape(bs, ts, ng * D_HEAD)
    og_s, rec_s = _gdn_sample(xs_rows, state_gdn_conv[l], ab_s, state_gdn_rec[l], w_conv[l], par,
                              gdn_norm[l], n_heads=n_heads)
    conv_p = jnp.stack([lax.slice(proj, ((b + 1) * tp - (CONV_W - 1), 0), ((b + 1) * tp, 3 * hd))
                        for b in range(bp)])
    conv_s = jnp.concatenate([state_gdn_conv[l], xs_rows[:, :, :3 * hd]], axis=1)[:, -(CONV_W - 1):]

    mix = _sb_prompt(proj, mix, sb_bias[l].astype(F32), sb_norm[l], nb=bp, t=tp, n_heads=n_sb,
                     q_col=sb0, k_col=sb0 + n_sb, v_col=sb0 + 2 * n_sb, o_col=n_heads)
    q_s = samp[:, 4 * hd:4 * hd + sbd].reshape(bs, ts, sbd)
    k_s = samp[:, 4 * hd + sbd:4 * hd + 2 * sbd].reshape(bs, ts, sbd)
    v_s = samp[:, 4 * hd + 2 * sbd:].reshape(bs, ts, sbd)
    n_phys = cache_sb_k.shape[1]
    ck = cache_sb_k[l].reshape(n_phys, PAGE_SIZE * n_sb, D_HEAD)
    cv = cache_sb_v[l].reshape(n_phys, PAGE_SIZE * n_sb, D_HEAD)
    os_s = _sb_sample(q_s, k_s, v_s, ck, cv, page_table, sb_bias[l], sb_norm[l])
    mix_s = jnp.concatenate([og_s.reshape(ms, hd), os_s.reshape(ms, sbd)], axis=1).astype(BF16)
    mix = lax.dynamic_update_slice(mix, mix_s, (mp, 0))
    k_p = proj[:mp, 4 * hd + sbd:4 * hd + 2 * sbd].reshape(bp, tp, n_sb, D_HEAD)
    v_p = proj[:mp, 4 * hd + 2 * sbd:4 * hd + 3 * sbd].reshape(bp, tp, n_sb, D_HEAD)

    x = _matmul(mix, w_out[l], d_model, tm=tm, res=x, name="out_proj")

    n_mem = mem_prompt.shape[1]
    md = w_mem_k.shape[-1]
    memn = _rmsnorm(mem_prompt.reshape(bp * n_mem, d_model), mem_norm[l], BF16, n_mem)
    mk = _matmul(memn, w_mem_k[l], md, tm=bp * n_mem, name="mem_k_proj")
    mv = _matmul(memn, w_mem_v[l], md, tm=bp * n_mem, name="mem_v_proj")
    h = _rmsnorm(x, norm_mem[l], BF16, tm)
    qm = _matmul(h, w_mem_q[l], md, tm=tm, out_dtype=BF16, name="mem_q_proj")
    om_p = _mem_attn(qm[:mp].reshape(bp, tp, md), mk.reshape(bp, n_mem, md), mv.reshape(bp, n_mem, md),
                     min(tp, 512), BF16)
    qm_s = jnp.pad(qm[mp:].astype(F32).reshape(bs, ts, md), ((0, 0), (0, SUBLANES - ts), (0, 0)))
    om_s = _mem_attn(qm_s, cache_mem_k[l].reshape(bs, n_mem, md), cache_mem_v[l].reshape(bs, n_mem, md),
                     SUBLANES, F32)[:, :ts]
    om = jnp.concatenate([om_p.reshape(mp, md), om_s.reshape(ms, md).astype(BF16)], axis=0)
    x = _matmul(om, w_mem_o[l], d_model, tm=tm, res=x, name="mem_o_proj")

    h = _rmsnorm(x, norm_mlp[l], BF16, tm)
    d_ff = w_up.shape[-1]
    hid = _matmul(h, w_up[l], d_ff, tm=tm, relu2=True, out_dtype=BF16, name="mlp_up")
    half = d_ff // 2
    x = _matmul(hid, w_down[l], d_model, tm=tm, k=half, res=x, w_single_buffer=True,
                name="mlp_down_lo")
    x = _matmul(hid, w_down[l], d_model, tm=tm, k=half, a_col_block=1, w_row_block=1, res=x,
                w_single_buffer=True, name="mlp_down_hi")

    n_mh = md // D_HEAD
    outs = dict(k_p=k_p, v_p=v_p, k_s=k_s.reshape(bs, ts, n_sb, D_HEAD),
                v_s=v_s.reshape(bs, ts, n_sb, D_HEAD), conv_p=conv_p, conv_s=conv_s,
                rec_p=rec_p, rec_s=rec_s,
                mk=mk.reshape(bp, n_mem, n_mh, D_HEAD), mv=mv.reshape(bp, n_mem, n_mh, D_HEAD))
    return x, outs


def kernel(x_prompt, x_sample, cache_sb_k, cache_sb_v, state_gdn_conv, state_gdn_rec, cache_mem_k,
           cache_mem_v, page_table, mem_prompt, norm_mix, w_in, w_conv, a_log, dt_bias, gdn_norm,
           sb_norm, sb_bias, w_out, norm_mem, mem_norm, w_mem_q, w_mem_k, w_mem_v, w_mem_o,
           norm_mlp, w_up, w_down, norm_final):
    bp, tp, d_model = x_prompt.shape
    bs, ts, _ = x_sample.shape
    mp, ms = bp * tp, bs * ts
    depth = w_in.shape[0]
    h, x = _rmsnorm_join(x_prompt.reshape(mp, d_model), x_sample.reshape(ms, d_model), norm_mix[0])
    per_layer = []
    for l in range(depth):
        if l > 0:
            h = _rmsnorm(x, norm_mix[l], BF16, ROW_TILE)
        x, outs = _layer(
            l, x, h, cache_sb_k, cache_sb_v, state_gdn_conv, state_gdn_rec, cache_mem_k,
            cache_mem_v, page_table, mem_prompt, w_in, w_conv, a_log, dt_bias, gdn_norm,
            sb_norm, sb_bias, w_out, norm_mem, mem_norm, w_mem_q, w_mem_k, w_mem_v, w_mem_o,
            norm_mlp, w_up, w_down, bp=bp, tp=tp, bs=bs, ts=ts)
        per_layer.append(outs)
    tile_p = 4 * ROW_TILE if mp % (4 * ROW_TILE) == 0 else ROW_TILE
    y_p = _rmsnorm(x, norm_final, F32, tile_p, rows=mp).reshape(bp, tp, d_model)
    y_s = _rmsnorm(x, norm_final, F32, ms, row_block_offset=mp // ms, rows=ms).reshape(bs, ts, d_model)

    def stack(name):
        return jnp.stack([o[name] for o in per_layer])

    return (y_p, y_s, stack("k_p"), stack("v_p"), stack("k_s"), stack("v_s"),
            stack("conv_p"), stack("conv_s"), stack("rec_p"), stack("rec_s"),
            stack("mk"), stack("mv"))
```

```python
import functools
import math

import jax
import jax.numpy as jnp
from jax import lax
from jax.experimental import pallas as pl
from jax.experimental.pallas import tpu as pltpu

F32 = jnp.float32
BF16 = jnp.bfloat16

EPS = 1e-6
D_HEAD = 128
SUBLANES = 8
GDN_CHUNK = 64
GDN_HEADS_PER_STEP = 8
GDN_PROMPT_ROWS = 256
CONV_W = 4
PAGE_SIZE = 128
SB_TQ = 512
SB_TK = 128
SB_PAGES_PER_STEP = 8
V7X_VMEM_LIMIT_BYTES = 56 * 1024 * 1024
MM_TN = 512
MM_TM = 640
ROW_TILE = 128


def _cparams(sem):
    return pltpu.CompilerParams(dimension_semantics=sem, vmem_limit_bytes=V7X_VMEM_LIMIT_BYTES)


def _softplus(x):
    return jnp.maximum(x, 0.0) + jnp.log1p(jnp.exp(-jnp.abs(x)))


def _sigmoid(x):
    return 1.0 / (1.0 + jnp.exp(-x))


def _dot(a, b):
    return jnp.dot(a, b, preferred_element_type=F32)


def _dot_nt(a, b):
    return lax.dot_general(a, b, (((1,), (1,)), ((), ())), preferred_element_type=F32)


def _split2(x):
    hi = x.astype(BF16)
    lo = (x - hi.astype(F32)).astype(BF16)
    return hi, lo


def _split3(x):
    hi = x.astype(BF16)
    r = x - hi.astype(F32)
    mid = r.astype(BF16)
    lo = (r - mid.astype(F32)).astype(BF16)
    return hi, mid, lo


def _dot_hp(a, b):
    a_hi, a_lo = _split2(a)
    b_hi, b_lo = _split2(b)
    return _dot(a_hi, b_hi) + (_dot(a_hi, b_lo) + _dot(a_lo, b_hi))


def _rms(x, g):
    y = x * lax.rsqrt(jnp.mean(x * x, axis=-1, keepdims=True) + EPS)
    return y * g


def _rmsnorm_body(x_ref, g_ref, o_ref):
    o_ref[...] = _rms(x_ref[...], g_ref[...]).astype(o_ref.dtype)


def _rmsnorm(x, g, out_dtype, tm, row_block_offset=0, rows=None):
    m, d = x.shape
    rows = m if rows is None else rows
    off = row_block_offset
    assert rows % tm == 0
    return pl.pallas_call(
        _rmsnorm_body,
        out_shape=jax.ShapeDtypeStruct((rows, d), out_dtype),
        grid=(rows // tm,),
        in_specs=[pl.BlockSpec((tm, d), lambda i: (i + off, 0)),
                  pl.BlockSpec((1, d), lambda i: (0, 0))],
        out_specs=pl.BlockSpec((tm, d), lambda i: (i, 0)),
        compiler_params=_cparams(("parallel",)),
        name="rmsnorm",
    )(x, g.reshape(1, d))


def _rmsnorm_join_body(xp_ref, xs_ref, g_ref, h_ref, x_ref, *, n_prompt_blocks):
    i = pl.program_id(0)

    @pl.when(i < n_prompt_blocks)
    def _():
        x = xp_ref[...]
        x_ref[...] = x
        h_ref[...] = _rms(x, g_ref[...]).astype(h_ref.dtype)

    @pl.when(i >= n_prompt_blocks)
    def _():
        x = xs_ref[...]
        x_ref[...] = x
        h_ref[...] = _rms(x, g_ref[...]).astype(h_ref.dtype)


def _rmsnorm_join(xp, xs, g):
    mp, d = xp.shape
    ms = xs.shape[0]
    tm = ms
    assert mp % tm == 0
    npb = mp // tm
    return pl.pallas_call(
        functools.partial(_rmsnorm_join_body, n_prompt_blocks=npb),
        out_shape=(jax.ShapeDtypeStruct((mp + ms, d), BF16),
                   jax.ShapeDtypeStruct((mp + ms, d), F32)),
        grid=(npb + 1,),
        in_specs=[pl.BlockSpec((tm, d), lambda i: (jnp.minimum(i, npb - 1), 0)),
                  pl.BlockSpec((tm, d), lambda i: (0, 0)),
                  pl.BlockSpec((1, d), lambda i: (0, 0))],
        out_specs=(pl.BlockSpec((tm, d), lambda i: (i, 0)),
                   pl.BlockSpec((tm, d), lambda i: (i, 0))),
        compiler_params=_cparams(("arbitrary",)),
        name="rmsnorm_join",
    )(xp, xs, g.reshape(1, d))


def _matmul_body(*refs, relu2, has_res):
    if has_res:
        a_ref, w_ref, r_ref, o_ref, wbf_ref = refs
    else:
        a_ref, w_ref, o_ref, wbf_ref = refs

    @pl.when(pl.program_id(1) == 0)
    def _():
        wbf_ref[...] = w_ref[...].astype(BF16)

    acc = _dot(a_ref[...], wbf_ref[...])
    if relu2:
        acc = jnp.square(jnp.maximum(acc, 0.0))
    if has_res:
        acc = acc + r_ref[...]
    o_ref[...] = acc.astype(o_ref.dtype)


def _matmul(a, w, n, *, tm, k=None, tn=MM_TN, a_col_block=0, w_row_block=0, w_col_block=0,
            res=None, relu2=False, out_dtype=F32, w_single_buffer=False, name="matmul"):
    m = a.shape[0]
    k = a.shape[1] if k is None else k
    while n % tn:
        tn //= 2
    assert m % tm == 0 and tn % D_HEAD == 0
    w_mode = dict(pipeline_mode=pl.Buffered(1)) if w_single_buffer else {}
    in_specs = [pl.BlockSpec((tm, k), lambda j, i: (i, a_col_block)),
                pl.BlockSpec((k, tn), lambda j, i: (w_row_block, j + w_col_block), **w_mode)]
    args = [a, w]
    if res is not None:
        in_specs.append(pl.BlockSpec((tm, tn), lambda j, i: (i, j)))
        args.append(res)
    return pl.pallas_call(
        functools.partial(_matmul_body, relu2=relu2, has_res=res is not None),
        out_shape=jax.ShapeDtypeStruct((m, n), out_dtype),
        grid=(n // tn, m // tm),
        in_specs=in_specs,
        out_specs=pl.BlockSpec((tm, tn), lambda j, i: (i, j)),
        scratch_shapes=[pltpu.VMEM((k, tn), BF16)],
        compiler_params=_cparams(("parallel", "arbitrary")),
        name=name,
    )(*args)


def _norm_matmul_body(x_ref, g_ref, w_ref, o_ref, wbf_ref):
    @pl.when(pl.program_id(0) == 0)
    def _():
        wbf_ref[...] = w_ref[...].astype(BF16)

    h = _rms(x_ref[...], g_ref[...]).astype(BF16)
    o_ref[...] = _dot(h, wbf_ref[...]).astype(o_ref.dtype)


def _norm_matmul(x, g, w, *, tm, out_dtype):
    m, k = x.shape
    n = w.shape[1]
    assert m % tm == 0
    return pl.pallas_call(
        _norm_matmul_body,
        out_shape=jax.ShapeDtypeStruct((m, n), out_dtype),
        grid=(m // tm,),
        in_specs=[pl.BlockSpec((tm, k), lambda i: (i, 0)),
                  pl.BlockSpec((1, k), lambda i: (0, 0)),
                  pl.BlockSpec((k, n), lambda i: (0, 0))],
        out_specs=pl.BlockSpec((tm, n), lambda i: (i, 0)),
        scratch_shapes=[pltpu.VMEM((k, n), BF16)],
        compiler_params=_cparams(("arbitrary",)),
        name="norm_matmul",
    )(x, g.reshape(1, k), w)


def _matmul_res_norm_body(a_ref, w_ref, r_ref, g_ref, x_ref, h_ref, wbf_ref):
    @pl.when(pl.program_id(0) == 0)
    def _():
        wbf_ref[...] = w_ref[...].astype(BF16)

    x = _dot(a_ref[...], wbf_ref[...]) + r_ref[...]
    x_ref[...] = x
    h_ref[...] = _rms(x, g_ref[...]).astype(h_ref.dtype)


def _matmul_res_norm(a, w, res, g, *, tm):
    m, k = a.shape
    n = w.shape[1]
    assert m % tm == 0
    return pl.pallas_call(
        _matmul_res_norm_body,
        out_shape=(jax.ShapeDtypeStruct((m, n), F32), jax.ShapeDtypeStruct((m, n), BF16)),
        grid=(m // tm,),
        in_specs=[pl.BlockSpec((tm, k), lambda i: (i, 0)),
                  pl.BlockSpec((k, n), lambda i: (0, 0)),
                  pl.BlockSpec((tm, n), lambda i: (i, 0)),
                  pl.BlockSpec((1, n), lambda i: (0, 0))],
        out_specs=(pl.BlockSpec((tm, n), lambda i: (i, 0)),
                   pl.BlockSpec((tm, n), lambda i: (i, 0))),
        scratch_shapes=[pltpu.VMEM((k, n), BF16)],
        compiler_params=_cparams(("arbitrary",)),
        name="matmul_res_norm",
    )(a, w, res, g.reshape(1, n))


def _matmul_nt_body(a_ref, w_ref, o_ref, wbf_ref):
    @pl.when(pl.program_id(1) == 0)
    def _():
        wbf_ref[...] = w_ref[...].astype(BF16)

    o_ref[...] = _dot_nt(a_ref[...], wbf_ref[...]).astype(o_ref.dtype)


def _matmul_nt(a, wt, row_segments, *, tm, tn=MM_TN, name="matmul_nt"):
    m, k = a.shape
    starts = []
    for first, n_rows in row_segments:
        assert first % SUBLANES == 0 and n_rows % tn == 0
        starts += [first + i * tn for i in range(n_rows // tn)]
    n = tn * len(starts)
    def w_row(j):
        row8 = j * (tn // SUBLANES)
        base, shift = 0, 0
        for first, n_rows in row_segments:
            step = (first - base * tn) - shift
            if step:
                row8 = row8 + jnp.where(j >= base, step // SUBLANES, 0)
            shift += step
            base += n_rows // tn
        return row8 * SUBLANES

    return pl.pallas_call(
        _matmul_nt_body,
        out_shape=jax.ShapeDtypeStruct((m, n), F32),
        grid=(n // tn, m // tm),
        in_specs=[pl.BlockSpec((tm, k), lambda j, i: (i, 0)),
                  pl.BlockSpec((pl.Element(tn), pl.Element(k)), lambda j, i: (w_row(j), 0))],
        out_specs=pl.BlockSpec((tm, tn), lambda j, i: (i, j)),
        scratch_shapes=[pltpu.VMEM((tn, k), BF16)],
        compiler_params=_cparams(("parallel", "arbitrary")),
        name=name,
    )(a, wt)


def _mem_attn_body(q_ref, k_ref, v_ref, o_ref, *, n_heads, scale):
    for h in range(n_heads):
        sl = slice(h * D_HEAD, (h + 1) * D_HEAD)
        q = q_ref[0, :, sl].astype(BF16)
        k = k_ref[0, :, sl].astype(BF16)
        v = v_ref[0, :, sl].astype(BF16)
        s = _dot_nt(q, k) * scale
        s = s - jnp.max(s, axis=-1, keepdims=True)
        e = jnp.exp(s)
        p = e / jnp.sum(e, axis=-1, keepdims=True)
        o_ref[0, :, sl] = _dot(p.astype(BF16), v).astype(o_ref.dtype)


def _mem_attn(q, k, v, tq, out_dtype):
    nb, t, hd = q.shape
    n_mem = k.shape[1]
    return pl.pallas_call(
        functools.partial(_mem_attn_body, n_heads=hd // D_HEAD, scale=D_HEAD ** -0.5),
        out_shape=jax.ShapeDtypeStruct((nb, t, hd), out_dtype),
        grid=(nb, t // tq),
        in_specs=[pl.BlockSpec((1, tq, hd), lambda b, i: (b, i, 0)),
                  pl.BlockSpec((1, n_mem, hd), lambda b, i: (b, 0, 0)),
                  pl.BlockSpec((1, n_mem, hd), lambda b, i: (b, 0, 0))],
        out_specs=pl.BlockSpec((1, tq, hd), lambda b, i: (b, i, 0)),
        compiler_params=_cparams(("parallel", "parallel")),
        name="mem_attn",
    )(q, k, v)


def _later_keys_neg(n):
    r = lax.broadcasted_iota(jnp.int32, (n, n), 0)
    c = lax.broadcasted_iota(jnp.int32, (n, n), 1)
    return jnp.where(r > c, -1.0, 0.0).astype(BF16)


LOG2_E = 1.4426950408889634


def _sb_blocks(zs, valids, c, u_neg, lead_cols=1):
    sps = [jnp.maximum(z, 0.0) + jnp.log2(1.0 + jnp.exp2(-jnp.abs(z))) for z in zs]
    stays = [sp if v is None else jnp.where(v, sp, 0.0) for sp, v in zip(sps, valids)]
    parts = [_split2(st) for st in stays]
    afters = [_dot(hi, u_neg) + _dot(lo, u_neg) for hi, lo in parts]
    out = []
    for z, sp, st, after, v in zip(zs, sps, stays, afters, valids):
        a = jnp.exp2((z - sp) + after + c)
        out.append(a if v is None else jnp.where(v, a, 0.0))
        if lead_cols:
            for j in range(lead_cols):
                c = c + (after[:, j:j + 1] - st[:, j:j + 1])
        else:
            c = c - jnp.sum(st, axis=-1, keepdims=True)
    return out, c


def _sb_prompt_body(bias_ref, q_ref, k_ref, v_ref, g_ref, mix_ref, o_ref, *, tq, tk, scale):
    del mix_ref
    h = pl.program_id(1)
    qi = pl.program_id(2)
    r = tq // tk
    bias = bias_ref[h] * LOG2_E
    scale = scale * LOG2_E
    q = q_ref[...].astype(BF16)
    u = _later_keys_neg(tk)

    def blocks(qrows, row0, j_hi, masks, c, acc):
        n = qrows.shape[0]
        k0s = [pl.multiple_of((j_hi - i) * tk, tk) for i in range(len(masks))]
        zs = [_dot_nt(qrows, k_ref[pl.ds(k0, tk), :].astype(BF16)) * scale + bias for k0 in k0s]
        row = lax.broadcasted_iota(jnp.int32, (n, tk), 0) + (qi * tq + row0)
        col = lax.broadcasted_iota(jnp.int32, (n, tk), 1)
        valids = [((k0 + col) < row) if m else None for k0, m in zip(k0s, masks)]
        ws, c = _sb_blocks(zs, valids, c, u)
        for w, k0 in zip(ws, k0s):
            acc = acc + _dot(w.astype(BF16), v_ref[pl.ds(k0, tk), :].astype(BF16))
        return c, acc

    hr = r // 2
    half = tq // 2
    zc = jnp.zeros((half, 1), F32)
    za = jnp.zeros((half, D_HEAD), F32)
    c_top, acc_top = blocks(q[:half], 0, qi * r + hr - 1, [True] * hr, zc, za)
    c_bot, acc_bot = blocks(q[half:], half, qi * r + r - 1, [True] * hr + [False] * hr, zc, za)
    c = jnp.concatenate([c_top, c_bot], axis=0)
    acc = jnp.concatenate([acc_top, acc_bot], axis=0)

    def body(it, carry):
        return blocks(q, 0, (qi - 1 - it) * r + (r - 1), [False] * r, *carry)

    c, acc = lax.fori_loop(0, qi, body, (c, acc))
    o_ref[...] = _rms(acc, g_ref[...]).astype(o_ref.dtype)


def _sb_prompt(qkv, mix, bias, gain, *, nb, t, n_heads, q_col, k_col, v_col, o_col):
    tq, tk = SB_TQ, SB_TK
    nq = t // tq
    return pl.pallas_call(
        functools.partial(_sb_prompt_body, tq=tq, tk=tk, scale=D_HEAD ** -0.5),
        out_shape=jax.ShapeDtypeStruct(mix.shape, mix.dtype),
        grid=(nb, n_heads, nq),
        in_specs=[pl.BlockSpec(memory_space=pltpu.SMEM),
                  pl.BlockSpec((tq, D_HEAD), lambda b, h, i: (b * nq + i, q_col + h)),
                  pl.BlockSpec((t, D_HEAD), lambda b, h, i: (b, k_col + h)),
                  pl.BlockSpec((t, D_HEAD), lambda b, h, i: (b, v_col + h)),
                  pl.BlockSpec((1, D_HEAD), lambda b, h, i: (0, 0)),
                  pl.BlockSpec(memory_space=pl.ANY)],
        out_specs=pl.BlockSpec((tq, D_HEAD), lambda b, h, i: (b * nq + i, o_col + h)),
        input_output_aliases={5: 0},
        compiler_params=_cparams(("parallel", "parallel", "arbitrary")),
        name="sb_prompt",
    )(bias, qkv, qkv, qkv, gain.reshape(1, D_HEAD), mix)


def _sb_sample_body(pt_ref, q_ref, kn_ref, vn_ref, *rest, t, n_heads, n_steps, pages_per_step, scale):
    del pt_ref
    pps = pages_per_step
    ck_refs = rest[:pps]
    cv_refs = rest[pps:2 * pps]
    bias_ref, g_ref, o_ref, qall_ref, c_ref, acc_ref, new_ref, kv16_ref = rest[2 * pps:]
    s = pl.program_id(1)
    n_pairs = n_heads // 2
    rp = 2 * SUBLANES
    rows = n_pairs * rp
    tk = PAGE_SIZE
    cols = 2 * tk
    rr = lax.broadcasted_iota(jnp.int32, (cols, cols), 0)
    cc = lax.broadcasted_iota(jnp.int32, (cols, cols), 1)
    u2 = jnp.where(jnp.logical_and((rr >> 1) > (cc >> 1), (rr & 1) == (cc & 1)), -1.0, 0.0).astype(BF16)
    qrow = lax.broadcasted_iota(jnp.int32, (rows, cols), 0)
    col = lax.broadcasted_iota(jnp.int32, (rows, cols), 1)
    own = ((qrow >> 3) & 1) == (col & 1)

    def process(get_ks, get_vs, valids):
        qb = [qall_ref[g * rp:(g + 1) * rp, :].astype(BF16) for g in range(n_pairs)]
        bias2 = bias_ref[...] * LOG2_E
        zs = [jnp.concatenate([_dot_nt(qb[g], get_k(g)) for g in range(n_pairs)], axis=0)
              * (scale * LOG2_E) + bias2 for get_k in get_ks]
        ws, c_new = _sb_blocks(zs, valids, c_ref[...], u2, lead_cols=0)
        ws = [w.astype(BF16) for w in ws]
        outs = []
        for g in range(n_pairs):
            sl = slice(g * rp, (g + 1) * rp)
            o = acc_ref[sl, :]
            for w, get_v in zip(ws, get_vs):
                o = o + _dot(w[sl, :], get_v(g))
            outs.append(o)
        c_ref[...] = c_new
        acc_ref[...] = jnp.concatenate(outs, axis=0)

    @pl.when(s == 0)
    def _():
        qall_ref[...] = jnp.zeros(qall_ref.shape, F32)
        for hd in range(n_heads):
            qall_ref[hd * SUBLANES:hd * SUBLANES + t, :] = q_ref[0, :, hd * D_HEAD:(hd + 1) * D_HEAD]
        c_ref[...] = jnp.zeros(c_ref.shape, F32)
        acc_ref[...] = jnp.zeros(acc_ref.shape, F32)
        new_ref[...] = jnp.zeros(new_ref.shape, F32)
        for hd in range(n_heads):
            for j in range(t):
                r = 2 * j + (hd & 1)
                new_ref[0, hd >> 1, r:r + 1, :] = kn_ref[0, j:j + 1, hd * D_HEAD:(hd + 1) * D_HEAD]
                new_ref[1, hd >> 1, r:r + 1, :] = vn_ref[0, j:j + 1, hd * D_HEAD:(hd + 1) * D_HEAD]
        qi = qrow & (SUBLANES - 1)
        valid = jnp.logical_and(own, jnp.logical_and((col >> 1) < qi, qi < t))
        process([lambda g: new_ref[0, g].astype(BF16)], [lambda g: new_ref[1, g].astype(BF16)], [valid])

    @pl.when(s > 0)
    def _():
        def pair(slot, ref):
            kv16_ref[slot] = pltpu.bitcast(ref[0].astype(BF16), jnp.uint32)

            def get(g):
                return pltpu.bitcast(kv16_ref[slot, pl.ds(g, tk, stride=n_pairs), :], BF16)
            return get

        process([pair(p, r) for p, r in enumerate(ck_refs)],
                [pair(pps + p, r) for p, r in enumerate(cv_refs)], [own] * pps)

    @pl.when(s == n_steps - 1)
    def _():
        for hd in range(n_heads):
            o = acc_ref[hd * SUBLANES:hd * SUBLANES + t, :]
            o_ref[0, :, hd * D_HEAD:(hd + 1) * D_HEAD] = _rms(o, g_ref[...])


def _sb_sample(q, k_new, v_new, cache_k, cache_v, page_table, bias, gain):
    nb, t, hd = q.shape
    n_heads = hd // D_HEAD
    n_pages = page_table.shape[1]
    assert t <= SUBLANES and n_heads % 2 == 0
    rows = n_heads * SUBLANES
    bias_rows = jnp.repeat(bias.astype(F32), SUBLANES).reshape(rows, 1)

    pps = SB_PAGES_PER_STEP
    while n_pages % pps:
        pps //= 2
    n_steps = n_pages // pps + 1

    def page_spec(p):
        def idx(b, s, pt):
            return (pt[b, n_pages - 1 - ((jnp.maximum(s, 1) - 1) * pps + p)], 0, 0)
        return pl.BlockSpec((1, PAGE_SIZE * n_heads, D_HEAD), idx)

    tok_spec = pl.BlockSpec((1, t, hd), lambda b, s, pt: (b, 0, 0))
    grid_spec = pltpu.PrefetchScalarGridSpec(
        num_scalar_prefetch=1,
        grid=(nb, n_steps),
        in_specs=[tok_spec, tok_spec, tok_spec]
                 + [page_spec(p) for p in range(pps)] + [page_spec(p) for p in range(pps)]
                 + [pl.BlockSpec((rows, 1), lambda b, s, pt: (0, 0)),
                    pl.BlockSpec((1, D_HEAD), lambda b, s, pt: (0, 0))],
        out_specs=pl.BlockSpec((1, t, hd), lambda b, s, pt: (b, 0, 0)),
        scratch_shapes=[pltpu.VMEM((rows, D_HEAD), F32),
                        pltpu.VMEM((rows, 1), F32),
                        pltpu.VMEM((rows, D_HEAD), F32),
                        pltpu.VMEM((2, n_heads // 2, 2 * PAGE_SIZE, D_HEAD), F32),
                        pltpu.VMEM((2 * pps, PAGE_SIZE * n_heads // 2, D_HEAD), jnp.uint32)],
    )
    return pl.pallas_call(
        functools.partial(_sb_sample_body, t=t, n_heads=n_heads, n_steps=n_steps,
                          pages_per_step=pps, scale=D_HEAD ** -0.5),
        out_shape=jax.ShapeDtypeStruct((nb, t, hd), F32),
        grid_spec=grid_spec,
        compiler_params=_cparams(("parallel", "arbitrary")),
        name="sb_sample",
    )(page_table, q, k_new, v_new, *([cache_k] * pps), *([cache_v] * pps), bias_rows,
      gain.reshape(1, D_HEAD))


def _unit_lower_inverses(ms, eye):
    ps = [-m for m in ms]
    ts = [eye + p for p in ps]
    for _ in range(int(math.log2(ms[0].shape[0])) - 1):
        ps = [_dot_hp(p, p) for p in ps]
        ts = [t + _dot_hp(t, p) for t, p in zip(ts, ps)]
    return ts


def _gdn_body(*refs, tb, tv, sample, n_tb):
    g_heads = GDN_HEADS_PER_STEP
    gw = g_heads * D_HEAD
    c_len = GDN_CHUNK
    if sample:
        (xq_ref, xk_ref, xv_ref, z_ref, pq_ref, pk_ref, pv_ref, ab_ref, wq_ref, wk_ref, wv_ref,
         par_ref, gn_ref, s0_ref, o_ref, rec_ref, ext_ref, qkv_ref, gate_ref, s_ref) = refs
    else:
        (xq_ref, xk_ref, xv_ref, z_ref, pq_ref, pk_ref, pv_ref, ab_ref, wq_ref, wk_ref, wv_ref,
         par_ref, gn_ref, o_ref, rec_ref, ext_ref, qkv_ref, gate_ref, s_ref) = refs
    ti = pl.program_id(2)

    @pl.when(ti == 0)
    def _():
        if sample:
            s_ref[...] = s0_ref[0]
        else:
            s_ref[...] = jnp.zeros(s_ref.shape, F32)

    for part, (x_ref, p_ref, w_ref) in enumerate(((xq_ref, pq_ref, wq_ref),
                                                   (xk_ref, pk_ref, wk_ref),
                                                   (xv_ref, pv_ref, wv_ref))):
        if sample:
            ext_ref[part, 0:SUBLANES, :] = jnp.zeros((SUBLANES, gw), F32)
            ext_ref[part, SUBLANES - (CONV_W - 1):SUBLANES, :] = p_ref[0]
            ext_ref[part, SUBLANES:SUBLANES + tb, :] = jnp.zeros((tb, gw), F32)
            ext_ref[part, SUBLANES:SUBLANES + tv, :] = x_ref[0]
        else:
            ext_ref[part, 0:SUBLANES, :] = jnp.where(ti > 0, p_ref[...], 0.0)
            ext_ref[part, SUBLANES:SUBLANES + tb, :] = x_ref[...]
        w = w_ref[...]
        base = SUBLANES - (CONV_W - 1)
        y = ext_ref[part, base:base + tb, :] * w[0:1, :]
        for i in range(1, CONV_W):
            y = y + ext_ref[part, base + i:base + i + tb, :] * w[i:i + 1, :]
        y = y * _sigmoid(y)
        if part < 2:
            for g in range(g_heads):
                sl = slice(g * D_HEAD, (g + 1) * D_HEAD)
                seg = y[:, sl]
                seg = seg * lax.rsqrt(jnp.sum(seg * seg, axis=-1, keepdims=True) + EPS)
                if part == 0:
                    seg = seg * (D_HEAD ** -0.5)
                qkv_ref[part, :, sl] = seg
        else:
            qkv_ref[part] = y

    if sample:
        gate_ref[0] = jnp.zeros((tb, D_HEAD), F32)
        gate_ref[0, 0:tv, :] = ab_ref[0]
        ab = gate_ref[0]
    else:
        ab = ab_ref[...]
    a_log = par_ref[0, 0:1, :]
    dt_bias = par_ref[0, 1:2, :]
    rowi = lax.broadcasted_iota(jnp.int32, (tb, D_HEAD), 0)
    lane = lax.broadcasted_iota(jnp.int32, (tb, D_HEAD), 1)
    live = rowi < tv
    gdec = -jnp.exp(a_log) * _softplus(ab + dt_bias)
    gate_ref[0] = jnp.where(jnp.logical_and(live, lane < g_heads), gdec, 0.0)
    gate_ref[1] = jnp.where(live, _sigmoid(ab), 0.0)

    ri = lax.broadcasted_iota(jnp.int32, (c_len, c_len), 0)
    ci = lax.broadcasted_iota(jnp.int32, (c_len, c_len), 1)
    incl = ri >= ci
    strict = ri > ci
    eye = jnp.where(ri == ci, 1.0, 0.0).astype(F32)
    tri = jnp.where(ri <= ci, 1.0, 0.0).astype(BF16)
    gn = gn_ref[...]

    def chunk(ck, carry):
        r0 = pl.multiple_of(ck * c_len, c_len)
        rs = pl.ds(r0, c_len)
        g_t = gate_ref[0, rs, :].T
        hi, mid, lo = _split3(g_t)
        gc_t = _dot(hi, tri) + (_dot(mid, tri) + _dot(lo, tri))
        gc = gc_t.T
        beta_all = gate_ref[1, rs, :]
        hs = range(g_heads)
        sls = [slice(g * D_HEAD, (g + 1) * D_HEAD) for g in hs]
        q = [qkv_ref[0, rs, sl] for sl in sls]
        k = [qkv_ref[1, rs, sl] for sl in sls]
        v = [qkv_ref[2, rs, sl] for sl in sls]
        s_old = [s_ref[g] for g in hs]
        if sample:
            zz = [z_ref[0, :, sl] for sl in sls]
        else:
            zz = [z_ref[rs, sl] for sl in sls]
        gcol = [gc[:, g:g + 1] for g in hs]
        grow = [gc_t[g:g + 1, :] for g in hs]
        glast = [gcol[g][c_len - 1:c_len, :] for g in hs]
        beta = [beta_all[:, g_heads + g:g_heads + g + 1] for g in hs]
        decay = [jnp.exp(jnp.where(incl, gcol[g] - grow[g], -jnp.inf)) for g in hs]
        eg = [jnp.exp(gcol[g]) for g in hs]
        kb = [k[g] * beta[g] for g in hs]
        vb = [v[g] * beta[g] for g in hs]
        k16 = [k[g].astype(BF16) for g in hs]
        m = [jnp.where(strict, _dot_nt(kb[g].astype(BF16), k16[g]) * decay[g], 0.0) for g in hs]
        qk = [jnp.where(incl, _dot_nt(q[g].astype(BF16), k16[g]) * decay[g], 0.0) for g in hs]
        tmat = _unit_lower_inverses(m, eye)
        rhs = [jnp.concatenate([vb[g], kb[g] * eg[g]], axis=1).astype(BF16) for g in hs]
        uw = [_dot(tmat[g].astype(BF16), rhs[g]) for g in hs]
        s16 = [s_old[g].astype(BF16) for g in hs]
        ws = [_dot(jnp.concatenate([uw[g][:, D_HEAD:], q[g] * eg[g]], axis=0).astype(BF16), s16[g])
              for g in hs]
        v_new = [uw[g][:, :D_HEAD] - ws[g][:c_len] for g in hs]
        v16 = [v_new[g].astype(BF16) for g in hs]
        o = [ws[g][c_len:] + _dot(qk[g].astype(BF16), v16[g]) for g in hs]
        kg_t = [(k[g] * jnp.exp(glast[g] - gcol[g])).T.astype(BF16) for g in hs]
        s_new = [s_old[g] * jnp.exp(glast[g]) + _dot(kg_t[g], v16[g]) for g in hs]
        out = [_rms(o[g], gn) for g in hs]
        if sample:
            out = [out[g][0:tv] * (zz[g] * _sigmoid(zz[g])) for g in hs]
            o_ref[0] = jnp.concatenate(out, axis=1)
        else:
            out = [out[g] * (zz[g] * _sigmoid(zz[g])) for g in hs]
            o_ref[rs, :] = jnp.concatenate(out, axis=1).astype(o_ref.dtype)
        for g in hs:
            s_ref[g] = s_new[g]
        return carry

    lax.fori_loop(0, tb // c_len, chunk, 0)

    @pl.when(ti == n_tb - 1)
    def _():
        rec_ref[0] = s_ref[...]


def _gdn_scratch(tb):
    g = GDN_HEADS_PER_STEP
    gw = g * D_HEAD
    return [pltpu.VMEM((3, tb + SUBLANES, gw), F32),
            pltpu.VMEM((3, tb, gw), F32),
            pltpu.VMEM((2, tb, D_HEAD), F32),
            pltpu.VMEM((g, D_HEAD, D_HEAD), F32)]


def _gdn_params(a_log, dt_bias, n_heads):
    g = GDN_HEADS_PER_STEP
    ng = n_heads // g
    par = jnp.zeros((ng, SUBLANES, D_HEAD), F32)
    par = par.at[:, 0, :g].set(a_log.reshape(ng, g).astype(F32))
    par = par.at[:, 1, :g].set(dt_bias.reshape(ng, g).astype(F32))
    return par


def _gdn_prompt(qkvz, gates, gate_col, w_conv, par, gn, *, nb, t, n_heads):
    g = GDN_HEADS_PER_STEP
    gw = g * D_HEAD
    ng = n_heads // g
    tb = GDN_PROMPT_ROWS
    n_tb = t // tb
    hd = n_heads * D_HEAD
    m = qkvz.shape[0]

    def cur(part):
        return pl.BlockSpec((tb, gw), lambda b, hg, ti: (b * n_tb + ti, part * ng + hg))

    def prev(part):
        return pl.BlockSpec(
            (SUBLANES, gw),
            lambda b, hg, ti: (jnp.maximum((b * n_tb + ti) * (tb // SUBLANES) - 1, 0), part * ng + hg))

    def wspec(part):
        return pl.BlockSpec((CONV_W, gw), lambda b, hg, ti: (0, part * ng + hg))

    in_specs = [cur(0), cur(1), cur(2), cur(3), prev(0), prev(1), prev(2),
                pl.BlockSpec((tb, D_HEAD), lambda b, hg, ti: (b * n_tb + ti, gate_col + hg)),
                wspec(0), wspec(1), wspec(2),
                pl.BlockSpec((1, SUBLANES, D_HEAD), lambda b, hg, ti: (hg, 0, 0)),
                pl.BlockSpec((1, D_HEAD), lambda b, hg, ti: (0, 0))]
    out_shape = (jax.ShapeDtypeStruct((m, 2 * hd), BF16),
                 jax.ShapeDtypeStruct((nb, n_heads, D_HEAD, D_HEAD), F32))
    out_specs = (pl.BlockSpec((tb, gw), lambda b, hg, ti: (b * n_tb + ti, hg)),
                 pl.BlockSpec((1, g, D_HEAD, D_HEAD), lambda b, hg, ti: (b, hg, 0, 0)))
    return pl.pallas_call(
        functools.partial(_gdn_body, tb=tb, tv=tb, sample=False, n_tb=n_tb),
        out_shape=out_shape,
        grid=(nb, ng, n_tb),
        in_specs=in_specs,
        out_specs=out_specs,
        scratch_shapes=_gdn_scratch(tb),
        compiler_params=_cparams(("parallel", "parallel", "arbitrary")),
        name="gdn_prompt",
    )(qkvz, qkvz, qkvz, qkvz, qkvz, qkvz, qkvz, gates, w_conv, w_conv, w_conv,
      par, gn.reshape(1, D_HEAD))


def _gdn_sample(x, conv_prev, ab, rec_prev, w_conv, par, gn, *, n_heads):
    nb, t, _ = x.shape
    g = GDN_HEADS_PER_STEP
    gw = g * D_HEAD
    ng = n_heads // g
    tb = GDN_CHUNK
    hd = n_heads * D_HEAD
    assert t <= tb

    def cur(part):
        return pl.BlockSpec((1, t, gw), lambda b, hg, ti: (b, 0, part * ng + hg))

    def prev(part):
        return pl.BlockSpec((1, CONV_W - 1, gw), lambda b, hg, ti: (b, 0, part * ng + hg))

    def wspec(part):
        return pl.BlockSpec((CONV_W, gw), lambda b, hg, ti: (0, part * ng + hg))

    in_specs = [cur(0), cur(1), cur(2), cur(3), prev(0), prev(1), prev(2),
                pl.BlockSpec((1, t, D_HEAD), lambda b, hg, ti: (b, 0, hg)),
                wspec(0), wspec(1), wspec(2),
                pl.BlockSpec((1, SUBLANES, D_HEAD), lambda b, hg, ti: (hg, 0, 0)),
                pl.BlockSpec((1, D_HEAD), lambda b, hg, ti: (0, 0)),
                pl.BlockSpec((1, g, D_HEAD, D_HEAD), lambda b, hg, ti: (b, hg, 0, 0))]
    out_shape = (jax.ShapeDtypeStruct((nb, t, hd), F32),
                 jax.ShapeDtypeStruct((nb, n_heads, D_HEAD, D_HEAD), F32))
    out_specs = (pl.BlockSpec((1, t, gw), lambda b, hg, ti: (b, 0, hg)),
                 pl.BlockSpec((1, g, D_HEAD, D_HEAD), lambda b, hg, ti: (b, hg, 0, 0)))
    return pl.pallas_call(
        functools.partial(_gdn_body, tb=tb, tv=t, sample=True, n_tb=1),
        out_shape=out_shape,
        grid=(nb, ng, 1),
        in_specs=in_specs,
        out_specs=out_specs,
        scratch_shapes=_gdn_scratch(tb),
        compiler_params=_cparams(("parallel", "parallel", "arbitrary")),
        name="gdn_sample",
    )(x, x, x, x, conv_prev, conv_prev, conv_prev, ab, w_conv, w_conv, w_conv,
      par, gn.reshape(1, D_HEAD), rec_prev)


def _layer(l, x, h, cache_sb_k, cache_sb_v, state_gdn_conv, state_gdn_rec, cache_mem_k,
           cache_mem_v, page_table, mem_prompt, w_in, w_conv, a_log, dt_bias, gdn_norm,
           sb_norm, sb_bias, w_out, norm_mem, mem_norm, w_mem_q, w_mem_k, w_mem_v, w_mem_o,
           norm_mlp, w_up, w_down, *, bp, tp, bs, ts):
    m, d_model = x.shape
    mp, ms = bp * tp, bs * ts
    n_heads = w_conv.shape[-1] // (3 * D_HEAD)
    hd = n_heads * D_HEAD
    n_sb = sb_bias.shape[-1]
    sbd = n_sb * D_HEAD
    gh = GDN_HEADS_PER_STEP
    ng = n_heads // gh
    tm = MM_TM if m % MM_TM == 0 else ROW_TILE

    w_in_t = jnp.swapaxes(w_in, 1, 2)[l]
    ab_col = 4 * hd
    sb_col = ab_col + 2 * n_heads
    tn_in = MM_TN
    while (4 * hd) % tn_in or (3 * sbd) % tn_in:
        tn_in //= 2
    proj = _matmul_nt(h, w_in_t, [(0, 4 * hd), (sb_col, 3 * sbd)], tm=tm, tn=tn_in, name="in_proj")
    w_a = w_in_t[ab_col:ab_col + n_heads].reshape(ng, gh, d_model)
    w_b = w_in_t[ab_col + n_heads:sb_col].reshape(ng, gh, d_model)
    w_ab = jnp.concatenate([w_a, w_b, jnp.zeros((ng, D_HEAD - 2 * gh, d_model), w_in_t.dtype)],
                           axis=1).reshape(ng * D_HEAD, d_model)
    gates = _matmul_nt(h, w_ab, [(0, ng * D_HEAD)], tm=tm, tn=ng * D_HEAD, name="in_proj_gates")
    sb0 = 4 * n_heads

    par = _gdn_params(a_log[l], dt_bias[l], n_heads)
    mix, rec_p = _gdn_prompt(proj, gates, 0, w_conv[l], par, gdn_norm[l],
                             nb=bp, t=tp, n_heads=n_heads)
    samp = proj[mp:]
    xs_rows = samp[:, :4 * hd].reshape(bs, ts, 4 * hd)
    ab_s = gates[mp:].reshape(bs, ts, ng * D_HEAD)
    og_s, rec_s = _gdn_sample(xs_rows, state_gdn_conv[l], ab_s, state_gdn_rec[l], w_conv[l], par,
                              gdn_norm[l], n_heads=n_heads)
    conv_p = jnp.stack([lax.slice(proj, ((b + 1) * tp - (CONV_W - 1), 0), ((b + 1) * tp, 3 * hd))
                        for b in range(bp)])
    conv_s = jnp.concatenate([state_gdn_conv[l], xs_rows[:, :, :3 * hd]], axis=1)[:, -(CONV_W - 1):]

    mix = _sb_prompt(proj, mix, sb_bias[l].astype(F32), sb_norm[l], nb=bp, t=tp, n_heads=n_sb,
                     q_col=sb0, k_col=sb0 + n_sb, v_col=sb0 + 2 * n_sb, o_col=n_heads)
    q_s = samp[:, 4 * hd:4 * hd + sbd].reshape(bs, ts, sbd)
    k_s = samp[:, 4 * hd + sbd:4 * hd + 2 * sbd].reshape(bs, ts, sbd)
    v_s = samp[:, 4 * hd + 2 * sbd:].reshape(bs, ts, sbd)
    n_phys = cache_sb_k.shape[1]
    ck = cache_sb_k[l].reshape(n_phys, PAGE_SIZE * n_sb, D_HEAD)
    cv = cache_sb_v[l].reshape(n_phys, PAGE_SIZE * n_sb, D_HEAD)
    os_s = _sb_sample(q_s, k_s, v_s, ck, cv, page_table, sb_bias[l], sb_norm[l])
    mix_s = jnp.concatenate([og_s.reshape(ms, hd), os_s.reshape(ms, sbd)], axis=1).astype(BF16)
    mix = lax.dynamic_update_slice(mix, mix_s, (mp, 0))
    k_p = proj[:mp, 4 * hd + sbd:4 * hd + 2 * sbd].reshape(bp, tp, n_sb, D_HEAD)
    v_p = proj[:mp, 4 * hd + 2 * sbd:4 * hd + 3 * sbd].reshape(bp, tp, n_sb, D_HEAD)

    x = _matmul(mix, w_out[l], d_model, tm=tm, res=x, name="out_proj")

    n_mem = mem_prompt.shape[1]
    md = w_mem_k.shape[-1]
    memn = _rmsnorm(mem_prompt.reshape(bp * n_mem, d_model), mem_norm[l], BF16, n_mem)
    mk = _matmul(memn, w_mem_k[l], md, tm=bp * n_mem, name="mem_k_proj")
    mv = _matmul(memn, w_mem_v[l], md, tm=bp * n_mem, name="mem_v_proj")
    qm = _norm_matmul(x, norm_mem[l], w_mem_q[l], tm=tm, out_dtype=BF16)
    om_p = _mem_attn(qm[:mp].reshape(bp, tp, md), mk.reshape(bp, n_mem, md), mv.reshape(bp, n_mem, md),
                     min(tp, 512), BF16)
    qm_s = jnp.pad(qm[mp:].astype(F32).reshape(bs, ts, md), ((0, 0), (0, SUBLANES - ts), (0, 0)))
    om_s = _mem_attn(qm_s, cache_mem_k[l].reshape(bs, n_mem, md), cache_mem_v[l].reshape(bs, n_mem, md),
                     SUBLANES, F32)[:, :ts]
    om = jnp.concatenate([om_p.reshape(mp, md), om_s.reshape(ms, md).astype(BF16)], axis=0)
    x, h = _matmul_res_norm(om, w_mem_o[l], x, norm_mlp[l], tm=tm // 2 if tm % 32 == 0 else tm)

    d_ff = w_up.shape[-1]
    hid = _matmul(h, w_up[l], d_ff, tm=tm, relu2=True, out_dtype=BF16, name="mlp_up")
    half = d_ff // 2
    x = _matmul(hid, w_down[l], d_model, tm=tm, k=half, res=x, w_single_buffer=True,
                name="mlp_down_lo")
    x = _matmul(hid, w_down[l], d_model, tm=tm, k=half, a_col_block=1, w_row_block=1, res=x,
                w_single_buffer=True, name="mlp_down_hi")

    n_mh = md // D_HEAD
    outs = dict(k_p=k_p, v_p=v_p, k_s=k_s.reshape(bs, ts, n_sb, D_HEAD),
                v_s=v_s.reshape(bs, ts, n_sb, D_HEAD), conv_p=conv_p, conv_s=conv_s,
                rec_p=rec_p, rec_s=rec_s,
                mk=mk.reshape(bp, n_mem, n_mh, D_HEAD), mv=mv.reshape(bp, n_mem, n_mh, D_HEAD))
    return x, outs


def kernel(x_prompt, x_sample, cache_sb_k, cache_sb_v, state_gdn_conv, state_gdn_rec, cache_mem_k,
           cache_mem_v, page_table, mem_prompt, norm_mix, w_in, w_conv, a_log, dt_bias, gdn_norm,
           sb_norm, sb_bias, w_out, norm_mem, mem_norm, w_mem_q, w_mem_k, w_mem_v, w_mem_o,
           norm_mlp, w_up, w_down, norm_final):
    bp, tp, d_model = x_prompt.shape
    bs, ts, _ = x_sample.shape
    mp, ms = bp * tp, bs * ts
    depth = w_in.shape[0]
    h, x = _rmsnorm_join(x_prompt.reshape(mp, d_model), x_sample.reshape(ms, d_model), norm_mix[0])
    per_layer = []
    for l in range(depth):
        if l > 0:
            h = _rmsnorm(x, norm_mix[l], BF16, ROW_TILE)
        x, outs = _layer(
            l, x, h, cache_sb_k, cache_sb_v, state_gdn_conv, state_gdn_rec, cache_mem_k,
            cache_mem_v, page_table, mem_prompt, w_in, w_conv, a_log, dt_bias, gdn_norm,
            sb_norm, sb_bias, w_out, norm_mem, mem_norm, w_mem_q, w_mem_k, w_mem_v, w_mem_o,
            norm_mlp, w_up, w_down, bp=bp, tp=tp, bs=bs, ts=ts)
        per_layer.append(outs)
    tile_p = 4 * ROW_TILE if mp % (4 * ROW_TILE) == 0 else ROW_TILE
    y_p = _rmsnorm(x, norm_final, F32, tile_p, rows=mp).reshape(bp, tp, d_model)
    y_s = _rmsnorm(x, norm_final, F32, ms, row_block_offset=mp // ms, rows=ms).reshape(bs, ts, d_model)

    def stack(name):
        return jnp.stack([o[name] for o in per_layer])

    return (y_p, y_s, stack("k_p"), stack("v_p"), stack("k_s"), stack("v_s"),
            stack("conv_p"), stack("conv_s"), stack("rec_p"), stack("rec_s"),
            stack("mk"), stack("mv"))
```

```python
import functools
import math

import jax
import jax.numpy as jnp
from jax import lax
from jax.experimental import pallas as pl
from jax.experimental.pallas import tpu as pltpu

F32 = jnp.float32
BF16 = jnp.bfloat16

EPS = 1e-6
D_HEAD = 128
SUBLANES = 8
GDN_CHUNK = 64
GDN_HEADS_PER_STEP = 16
GDN_PROMPT_ROWS = 256
CONV_W = 4
PAGE_SIZE = 128
SB_TQ = 512
SB_TK = 128
SB_PAGES_PER_STEP = 8
V7X_VMEM_LIMIT_BYTES = 56 * 1024 * 1024
MM_TN = 512
MM_TM = 640
ROW_TILE = 128


def _cparams(sem):
    return pltpu.CompilerParams(dimension_semantics=sem, vmem_limit_bytes=V7X_VMEM_LIMIT_BYTES)


def _softplus(x):
    return jnp.maximum(x, 0.0) + jnp.log1p(jnp.exp(-jnp.abs(x)))


def _sigmoid(x):
    return 1.0 / (1.0 + jnp.exp(-x))


def _dot(a, b):
    return jnp.dot(a, b, preferred_element_type=F32)


def _dot_nt(a, b):
    return lax.dot_general(a, b, (((1,), (1,)), ((), ())), preferred_element_type=F32)


def _split2(x):
    hi = x.astype(BF16)
    lo = (x - hi.astype(F32)).astype(BF16)
    return hi, lo


def _split3(x):
    hi = x.astype(BF16)
    r = x - hi.astype(F32)
    mid = r.astype(BF16)
    lo = (r - mid.astype(F32)).astype(BF16)
    return hi, mid, lo


def _dot_hp(a, b):
    a_hi, a_lo = _split2(a)
    b_hi, b_lo = _split2(b)
    return _dot(a_hi, b_hi) + (_dot(a_hi, b_lo) + _dot(a_lo, b_hi))


def _rms(x, g):
    y = x * lax.rsqrt(jnp.mean(x * x, axis=-1, keepdims=True) + EPS)
    return y * g


def _rmsnorm_body(x_ref, g_ref, o_ref):
    o_ref[...] = _rms(x_ref[...], g_ref[...]).astype(o_ref.dtype)


def _rmsnorm(x, g, out_dtype, tm, row_block_offset=0, rows=None):
    m, d = x.shape
    rows = m if rows is None else rows
    off = row_block_offset
    assert rows % tm == 0
    return pl.pallas_call(
        _rmsnorm_body,
        out_shape=jax.ShapeDtypeStruct((rows, d), out_dtype),
        grid=(rows // tm,),
        in_specs=[pl.BlockSpec((tm, d), lambda i: (i + off, 0)),
                  pl.BlockSpec((1, d), lambda i: (0, 0))],
        out_specs=pl.BlockSpec((tm, d), lambda i: (i, 0)),
        compiler_params=_cparams(("parallel",)),
        name="rmsnorm",
    )(x, g.reshape(1, d))


def _rmsnorm_join_body(xp_ref, xs_ref, g_ref, h_ref, x_ref, *, n_prompt_blocks):
    i = pl.program_id(0)

    @pl.when(i < n_prompt_blocks)
    def _():
        x = xp_ref[...]
        x_ref[...] = x
        h_ref[...] = _rms(x, g_ref[...]).astype(h_ref.dtype)

    @pl.when(i >= n_prompt_blocks)
    def _():
        x = xs_ref[...]
        x_ref[...] = x
        h_ref[...] = _rms(x, g_ref[...]).astype(h_ref.dtype)


def _rmsnorm_join(xp, xs, g):
    mp, d = xp.shape
    ms = xs.shape[0]
    tm = ms
    assert mp % tm == 0
    npb = mp // tm
    return pl.pallas_call(
        functools.partial(_rmsnorm_join_body, n_prompt_blocks=npb),
        out_shape=(jax.ShapeDtypeStruct((mp + ms, d), BF16),
                   jax.ShapeDtypeStruct((mp + ms, d), F32)),
        grid=(npb + 1,),
        in_specs=[pl.BlockSpec((tm, d), lambda i: (jnp.minimum(i, npb - 1), 0)),
                  pl.BlockSpec((tm, d), lambda i: (0, 0)),
                  pl.BlockSpec((1, d), lambda i: (0, 0))],
        out_specs=(pl.BlockSpec((tm, d), lambda i: (i, 0)),
                   pl.BlockSpec((tm, d), lambda i: (i, 0))),
        compiler_params=_cparams(("arbitrary",)),
        name="rmsnorm_join",
    )(xp, xs, g.reshape(1, d))


def _matmul_body(*refs, relu2, has_res):
    if has_res:
        a_ref, w_ref, r_ref, o_ref, wbf_ref = refs
    else:
        a_ref, w_ref, o_ref, wbf_ref = refs

    @pl.when(pl.program_id(1) == 0)
    def _():
        wbf_ref[...] = w_ref[...].astype(BF16)

    acc = _dot(a_ref[...], wbf_ref[...])
    if relu2:
        acc = jnp.square(jnp.maximum(acc, 0.0))
    if has_res:
        acc = acc + r_ref[...]
    o_ref[...] = acc.astype(o_ref.dtype)


def _matmul(a, w, n, *, tm, k=None, tn=MM_TN, a_col_block=0, w_row_block=0, w_col_block=0,
            res=None, relu2=False, out_dtype=F32, w_single_buffer=False, name="matmul"):
    m = a.shape[0]
    k = a.shape[1] if k is None else k
    while n % tn:
        tn //= 2
    assert m % tm == 0 and tn % D_HEAD == 0
    w_mode = dict(pipeline_mode=pl.Buffered(1)) if w_single_buffer else {}
    in_specs = [pl.BlockSpec((tm, k), lambda j, i: (i, a_col_block)),
                pl.BlockSpec((k, tn), lambda j, i: (w_row_block, j + w_col_block), **w_mode)]
    args = [a, w]
    if res is not None:
        in_specs.append(pl.BlockSpec((tm, tn), lambda j, i: (i, j)))
        args.append(res)
    return pl.pallas_call(
        functools.partial(_matmul_body, relu2=relu2, has_res=res is not None),
        out_shape=jax.ShapeDtypeStruct((m, n), out_dtype),
        grid=(n // tn, m // tm),
        in_specs=in_specs,
        out_specs=pl.BlockSpec((tm, tn), lambda j, i: (i, j)),
        scratch_shapes=[pltpu.VMEM((k, tn), BF16)],
        compiler_params=_cparams(("parallel", "arbitrary")),
        name=name,
    )(*args)


def _norm_matmul_body(x_ref, g_ref, w_ref, o_ref, wbf_ref):
    @pl.when(pl.program_id(0) == 0)
    def _():
        wbf_ref[...] = w_ref[...].astype(BF16)

    h = _rms(x_ref[...], g_ref[...]).astype(BF16)
    o_ref[...] = _dot(h, wbf_ref[...]).astype(o_ref.dtype)


def _norm_matmul(x, g, w, *, tm, out_dtype):
    m, k = x.shape
    n = w.shape[1]
    assert m % tm == 0
    return pl.pallas_call(
        _norm_matmul_body,
        out_shape=jax.ShapeDtypeStruct((m, n), out_dtype),
        grid=(m // tm,),
        in_specs=[pl.BlockSpec((tm, k), lambda i: (i, 0)),
                  pl.BlockSpec((1, k), lambda i: (0, 0)),
                  pl.BlockSpec((k, n), lambda i: (0, 0))],
        out_specs=pl.BlockSpec((tm, n), lambda i: (i, 0)),
        scratch_shapes=[pltpu.VMEM((k, n), BF16)],
        compiler_params=_cparams(("arbitrary",)),
        name="norm_matmul",
    )(x, g.reshape(1, k), w)


def _matmul_res_norm_body(a_ref, w_ref, r_ref, g_ref, x_ref, h_ref, wbf_ref):
    @pl.when(pl.program_id(0) == 0)
    def _():
        wbf_ref[...] = w_ref[...].astype(BF16)

    x = _dot(a_ref[...], wbf_ref[...]) + r_ref[...]
    x_ref[...] = x
    h_ref[...] = _rms(x, g_ref[...]).astype(h_ref.dtype)


def _matmul_res_norm(a, w, res, g, *, tm):
    m, k = a.shape
    n = w.shape[1]
    assert m % tm == 0
    return pl.pallas_call(
        _matmul_res_norm_body,
        out_shape=(jax.ShapeDtypeStruct((m, n), F32), jax.ShapeDtypeStruct((m, n), BF16)),
        grid=(m // tm,),
        in_specs=[pl.BlockSpec((tm, k), lambda i: (i, 0)),
                  pl.BlockSpec((k, n), lambda i: (0, 0)),
                  pl.BlockSpec((tm, n), lambda i: (i, 0)),
                  pl.BlockSpec((1, n), lambda i: (0, 0))],
        out_specs=(pl.BlockSpec((tm, n), lambda i: (i, 0)),
                   pl.BlockSpec((tm, n), lambda i: (i, 0))),
        scratch_shapes=[pltpu.VMEM((k, n), BF16)],
        compiler_params=_cparams(("arbitrary",)),
        name="matmul_res_norm",
    )(a, w, res, g.reshape(1, n))


def _matmul_nt_body(a_ref, w_ref, o_ref, wbf_ref):
    @pl.when(pl.program_id(1) == 0)
    def _():
        wbf_ref[...] = w_ref[...].astype(BF16)

    o_ref[...] = _dot_nt(a_ref[...], wbf_ref[...]).astype(o_ref.dtype)


def _matmul_nt(a, wt, row_segments, *, tm, tn=MM_TN, name="matmul_nt"):
    m, k = a.shape
    starts = []
    for first, n_rows in row_segments:
        assert first % SUBLANES == 0 and n_rows % tn == 0
        starts += [first + i * tn for i in range(n_rows // tn)]
    n = tn * len(starts)
    def w_row(j):
        row8 = j * (tn // SUBLANES)
        base, shift = 0, 0
        for first, n_rows in row_segments:
            step = (first - base * tn) - shift
            if step:
                row8 = row8 + jnp.where(j >= base, step // SUBLANES, 0)
            shift += step
            base += n_rows // tn
        return row8 * SUBLANES

    return pl.pallas_call(
        _matmul_nt_body,
        out_shape=jax.ShapeDtypeStruct((m, n), F32),
        grid=(n // tn, m // tm),
        in_specs=[pl.BlockSpec((tm, k), lambda j, i: (i, 0)),
                  pl.BlockSpec((pl.Element(tn), pl.Element(k)), lambda j, i: (w_row(j), 0))],
        out_specs=pl.BlockSpec((tm, tn), lambda j, i: (i, j)),
        scratch_shapes=[pltpu.VMEM((tn, k), BF16)],
        compiler_params=_cparams(("parallel", "arbitrary")),
        name=name,
    )(a, wt)


def _mem_attn_body(q_ref, k_ref, v_ref, o_ref, *, n_heads, scale):
    for h in range(n_heads):
        sl = slice(h * D_HEAD, (h + 1) * D_HEAD)
        q = q_ref[0, :, sl].astype(BF16)
        k = k_ref[0, :, sl].astype(BF16)
        v = v_ref[0, :, sl].astype(BF16)
        s = _dot_nt(q, k) * scale
        s = s - jnp.max(s, axis=-1, keepdims=True)
        e = jnp.exp(s)
        p = e / jnp.sum(e, axis=-1, keepdims=True)
        o_ref[0, :, sl] = _dot(p.astype(BF16), v).astype(o_ref.dtype)


def _mem_attn(q, k, v, tq, out_dtype):
    nb, t, hd = q.shape
    n_mem = k.shape[1]
    return pl.pallas_call(
        functools.partial(_mem_attn_body, n_heads=hd // D_HEAD, scale=D_HEAD ** -0.5),
        out_shape=jax.ShapeDtypeStruct((nb, t, hd), out_dtype),
        grid=(nb, t // tq),
        in_specs=[pl.BlockSpec((1, tq, hd), lambda b, i: (b, i, 0)),
                  pl.BlockSpec((1, n_mem, hd), lambda b, i: (b, 0, 0)),
                  pl.BlockSpec((1, n_mem, hd), lambda b, i: (b, 0, 0))],
        out_specs=pl.BlockSpec((1, tq, hd), lambda b, i: (b, i, 0)),
        compiler_params=_cparams(("parallel", "parallel")),
        name="mem_attn",
    )(q, k, v)


def _later_keys_neg(n):
    r = lax.broadcasted_iota(jnp.int32, (n, n), 0)
    c = lax.broadcasted_iota(jnp.int32, (n, n), 1)
    return jnp.where(r > c, -1.0, 0.0).astype(BF16)


LOG2_E = 1.4426950408889634


def _sb_blocks(zs, valids, c, u_neg, lead_cols=1):
    sps = [jnp.maximum(z, 0.0) + jnp.log2(1.0 + jnp.exp2(-jnp.abs(z))) for z in zs]
    stays = [sp if v is None else jnp.where(v, sp, 0.0) for sp, v in zip(sps, valids)]
    parts = [_split2(st) for st in stays]
    afters = [_dot(hi, u_neg) + _dot(lo, u_neg) for hi, lo in parts]
    out = []
    for z, sp, st, after, v in zip(zs, sps, stays, afters, valids):
        a = jnp.exp2((z - sp) + after + c)
        out.append(a if v is None else jnp.where(v, a, 0.0))
        if lead_cols:
            for j in range(lead_cols):
                c = c + (after[:, j:j + 1] - st[:, j:j + 1])
        else:
            c = c - jnp.sum(st, axis=-1, keepdims=True)
    return out, c


def _sb_prompt_body(bias_ref, q_ref, k_ref, v_ref, g_ref, mix_ref, o_ref, *, tq, tk, scale):
    del mix_ref
    h = pl.program_id(1)
    qi = pl.program_id(2)
    r = tq // tk
    bias = bias_ref[h] * LOG2_E
    scale = scale * LOG2_E
    q = q_ref[...].astype(BF16)
    u = _later_keys_neg(tk)

    def blocks(qrows, row0, j_hi, masks, c, acc):
        n = qrows.shape[0]
        k0s = [pl.multiple_of((j_hi - i) * tk, tk) for i in range(len(masks))]
        zs = [_dot_nt(qrows, k_ref[pl.ds(k0, tk), :].astype(BF16)) * scale + bias for k0 in k0s]
        row = lax.broadcasted_iota(jnp.int32, (n, tk), 0) + (qi * tq + row0)
        col = lax.broadcasted_iota(jnp.int32, (n, tk), 1)
        valids = [((k0 + col) < row) if m else None for k0, m in zip(k0s, masks)]
        ws, c = _sb_blocks(zs, valids, c, u)
        for w, k0 in zip(ws, k0s):
            acc = acc + _dot(w.astype(BF16), v_ref[pl.ds(k0, tk), :].astype(BF16))
        return c, acc

    hr = r // 2
    half = tq // 2
    zc = jnp.zeros((half, 1), F32)
    za = jnp.zeros((half, D_HEAD), F32)
    c_top, acc_top = blocks(q[:half], 0, qi * r + hr - 1, [True] * hr, zc, za)
    c_bot, acc_bot = blocks(q[half:], half, qi * r + r - 1, [True] * hr + [False] * hr, zc, za)
    c = jnp.concatenate([c_top, c_bot], axis=0)
    acc = jnp.concatenate([acc_top, acc_bot], axis=0)

    def body(it, carry):
        return blocks(q, 0, (qi - 1 - it) * r + (r - 1), [False] * r, *carry)

    c, acc = lax.fori_loop(0, qi, body, (c, acc))
    o_ref[...] = _rms(acc, g_ref[...]).astype(o_ref.dtype)


def _sb_prompt(qkv, mix, bias, gain, *, nb, t, n_heads, q_col, k_col, v_col, o_col):
    tq, tk = SB_TQ, SB_TK
    nq = t // tq
    return pl.pallas_call(
        functools.partial(_sb_prompt_body, tq=tq, tk=tk, scale=D_HEAD ** -0.5),
        out_shape=jax.ShapeDtypeStruct(mix.shape, mix.dtype),
        grid=(nb, n_heads, nq),
        in_specs=[pl.BlockSpec(memory_space=pltpu.SMEM),
                  pl.BlockSpec((tq, D_HEAD), lambda b, h, i: (b * nq + i, q_col + h)),
                  pl.BlockSpec((t, D_HEAD), lambda b, h, i: (b, k_col + h)),
                  pl.BlockSpec((t, D_HEAD), lambda b, h, i: (b, v_col + h)),
                  pl.BlockSpec((1, D_HEAD), lambda b, h, i: (0, 0)),
                  pl.BlockSpec(memory_space=pl.ANY)],
        out_specs=pl.BlockSpec((tq, D_HEAD), lambda b, h, i: (b * nq + i, o_col + h)),
        input_output_aliases={5: 0},
        compiler_params=_cparams(("parallel", "parallel", "arbitrary")),
        name="sb_prompt",
    )(bias, qkv, qkv, qkv, gain.reshape(1, D_HEAD), mix)


def _sb_sample_body(pt_ref, q_ref, kn_ref, vn_ref, *rest, t, n_heads, n_steps, pages_per_step, scale):
    del pt_ref
    pps = pages_per_step
    ck_refs = rest[:pps]
    cv_refs = rest[pps:2 * pps]
    bias_ref, g_ref, o_ref, qall_ref, c_ref, acc_ref, new_ref, kv16_ref = rest[2 * pps:]
    s = pl.program_id(1)
    n_pairs = n_heads // 2
    rp = 2 * SUBLANES
    rows = n_pairs * rp
    tk = PAGE_SIZE
    cols = 2 * tk
    rr = lax.broadcasted_iota(jnp.int32, (cols, cols), 0)
    cc = lax.broadcasted_iota(jnp.int32, (cols, cols), 1)
    u2 = jnp.where(jnp.logical_and((rr >> 1) > (cc >> 1), (rr & 1) == (cc & 1)), -1.0, 0.0).astype(BF16)
    qrow = lax.broadcasted_iota(jnp.int32, (rows, cols), 0)
    col = lax.broadcasted_iota(jnp.int32, (rows, cols), 1)
    own = ((qrow >> 3) & 1) == (col & 1)

    def process(get_ks, get_vs, valids):
        qb = [qall_ref[g * rp:(g + 1) * rp, :].astype(BF16) for g in range(n_pairs)]
        bias2 = bias_ref[...] * LOG2_E
        zs = [jnp.concatenate([_dot_nt(qb[g], get_k(g)) for g in range(n_pairs)], axis=0)
              * (scale * LOG2_E) + bias2 for get_k in get_ks]
        ws, c_new = _sb_blocks(zs, valids, c_ref[...], u2, lead_cols=0)
        ws = [w.astype(BF16) for w in ws]
        outs = []
        for g in range(n_pairs):
            sl = slice(g * rp, (g + 1) * rp)
            o = acc_ref[sl, :]
            for w, get_v in zip(ws, get_vs):
                o = o + _dot(w[sl, :], get_v(g))
            outs.append(o)
        c_ref[...] = c_new
        acc_ref[...] = jnp.concatenate(outs, axis=0)

    @pl.when(s == 0)
    def _():
        qall_ref[...] = jnp.zeros(qall_ref.shape, F32)
        for hd in range(n_heads):
            qall_ref[hd * SUBLANES:hd * SUBLANES + t, :] = q_ref[0, :, hd * D_HEAD:(hd + 1) * D_HEAD]
        c_ref[...] = jnp.zeros(c_ref.shape, F32)
        acc_ref[...] = jnp.zeros(acc_ref.shape, F32)
        new_ref[...] = jnp.zeros(new_ref.shape, F32)
        for hd in range(n_heads):
            for j in range(t):
                r = 2 * j + (hd & 1)
                new_ref[0, hd >> 1, r:r + 1, :] = kn_ref[0, j:j + 1, hd * D_HEAD:(hd + 1) * D_HEAD]
                new_ref[1, hd >> 1, r:r + 1, :] = vn_ref[0, j:j + 1, hd * D_HEAD:(hd + 1) * D_HEAD]
        qi = qrow & (SUBLANES - 1)
        valid = jnp.logical_and(own, jnp.logical_and((col >> 1) < qi, qi < t))
        process([lambda g: new_ref[0, g].astype(BF16)], [lambda g: new_ref[1, g].astype(BF16)], [valid])

    @pl.when(s > 0)
    def _():
        def pair(slot, ref):
            kv16_ref[slot] = pltpu.bitcast(ref[0].astype(BF16), jnp.uint32)

            def get(g):
                return pltpu.bitcast(kv16_ref[slot, pl.ds(g, tk, stride=n_pairs), :], BF16)
            return get

        process([pair(p, r) for p, r in enumerate(ck_refs)],
                [pair(pps + p, r) for p, r in enumerate(cv_refs)], [own] * pps)

    @pl.when(s == n_steps - 1)
    def _():
        for hd in range(n_heads):
            o = acc_ref[hd * SUBLANES:hd * SUBLANES + t, :]
            o_ref[0, :, hd * D_HEAD:(hd + 1) * D_HEAD] = _rms(o, g_ref[...])


def _sb_sample(q, k_new, v_new, cache_k, cache_v, page_table, bias, gain):
    nb, t, hd = q.shape
    n_heads = hd // D_HEAD
    n_pages = page_table.shape[1]
    assert t <= SUBLANES and n_heads % 2 == 0
    rows = n_heads * SUBLANES
    bias_rows = jnp.repeat(bias.astype(F32), SUBLANES).reshape(rows, 1)

    pps = SB_PAGES_PER_STEP
    while n_pages % pps:
        pps //= 2
    n_steps = n_pages // pps + 1

    def page_spec(p):
        def idx(b, s, pt):
            return (pt[b, n_pages - 1 - ((jnp.maximum(s, 1) - 1) * pps + p)], 0, 0)
        return pl.BlockSpec((1, PAGE_SIZE * n_heads, D_HEAD), idx)

    tok_spec = pl.BlockSpec((1, t, hd), lambda b, s, pt: (b, 0, 0))
    grid_spec = pltpu.PrefetchScalarGridSpec(
        num_scalar_prefetch=1,
        grid=(nb, n_steps),
        in_specs=[tok_spec, tok_spec, tok_spec]
                 + [page_spec(p) for p in range(pps)] + [page_spec(p) for p in range(pps)]
                 + [pl.BlockSpec((rows, 1), lambda b, s, pt: (0, 0)),
                    pl.BlockSpec((1, D_HEAD), lambda b, s, pt: (0, 0))],
        out_specs=pl.BlockSpec((1, t, hd), lambda b, s, pt: (b, 0, 0)),
        scratch_shapes=[pltpu.VMEM((rows, D_HEAD), F32),
                        pltpu.VMEM((rows, 1), F32),
                        pltpu.VMEM((rows, D_HEAD), F32),
                        pltpu.VMEM((2, n_heads // 2, 2 * PAGE_SIZE, D_HEAD), F32),
                        pltpu.VMEM((2 * pps, PAGE_SIZE * n_heads // 2, D_HEAD), jnp.uint32)],
    )
    return pl.pallas_call(
        functools.partial(_sb_sample_body, t=t, n_heads=n_heads, n_steps=n_steps,
                          pages_per_step=pps, scale=D_HEAD ** -0.5),
        out_shape=jax.ShapeDtypeStruct((nb, t, hd), F32),
        grid_spec=grid_spec,
        compiler_params=_cparams(("parallel", "arbitrary")),
        name="sb_sample",
    )(page_table, q, k_new, v_new, *([cache_k] * pps), *([cache_v] * pps), bias_rows,
      gain.reshape(1, D_HEAD))


def _unit_lower_inverses(ms, eye):
    ps = [-m for m in ms]
    ts = [eye + p for p in ps]
    for rnd in range(int(math.log2(ms[0].shape[0])) - 1):
        if rnd == 0:
            ps = [_dot_hp(p, p) for p in ps]
            ts = [t + _dot_hp(t, p) for t, p in zip(ts, ps)]
        else:
            p16 = [p.astype(BF16) for p in ps]
            ps = [_dot(p, p) for p in p16]
            ts = [t + _dot(t.astype(BF16), p.astype(BF16)) for t, p in zip(ts, ps)]
    return ts


def _gdn_body(*refs, tb, tv, sample, n_tb):
    g_heads = GDN_HEADS_PER_STEP
    gw = g_heads * D_HEAD
    c_len = GDN_CHUNK
    if sample:
        (xq_ref, xk_ref, xv_ref, z_ref, pq_ref, pk_ref, pv_ref, ab_ref, wq_ref, wk_ref, wv_ref,
         par_ref, gn_ref, s0_ref, o_ref, rec_ref, ext_ref, qkv_ref, gate_ref, s_ref) = refs
    else:
        (xq_ref, xk_ref, xv_ref, z_ref, pq_ref, pk_ref, pv_ref, ab_ref, wq_ref, wk_ref, wv_ref,
         par_ref, gn_ref, o_ref, rec_ref, ext_ref, qkv_ref, gate_ref, s_ref) = refs
    ti = pl.program_id(2)

    @pl.when(ti == 0)
    def _():
        if sample:
            s_ref[...] = s0_ref[0]
        else:
            s_ref[...] = jnp.zeros(s_ref.shape, F32)

    for part, (x_ref, p_ref, w_ref) in enumerate(((xq_ref, pq_ref, wq_ref),
                                                   (xk_ref, pk_ref, wk_ref),
                                                   (xv_ref, pv_ref, wv_ref))):
        if sample:
            ext_ref[part, 0:SUBLANES, :] = jnp.zeros((SUBLANES, gw), F32)
            ext_ref[part, SUBLANES - (CONV_W - 1):SUBLANES, :] = p_ref[0]
            ext_ref[part, SUBLANES:SUBLANES + tb, :] = jnp.zeros((tb, gw), F32)
            ext_ref[part, SUBLANES:SUBLANES + tv, :] = x_ref[0]
        else:
            ext_ref[part, 0:SUBLANES, :] = jnp.where(ti > 0, p_ref[...], 0.0)
            ext_ref[part, SUBLANES:SUBLANES + tb, :] = x_ref[...]
        base = SUBLANES - (CONV_W - 1)
        for g in range(g_heads):
            sl = slice(g * D_HEAD, (g + 1) * D_HEAD)
            w = w_ref[:, sl]
            for r0 in range(0, tb, c_len):
                y = ext_ref[part, base + r0:base + r0 + c_len, sl] * w[0:1, :]
                for i in range(1, CONV_W):
                    y = y + ext_ref[part, base + r0 + i:base + r0 + i + c_len, sl] * w[i:i + 1, :]
                y = y / (1.0 + jnp.exp2(y * (-LOG2_E)))
                if part < 2:
                    inv = lax.rsqrt(jnp.sum(y * y, axis=-1, keepdims=True) + EPS)
                    if part == 0:
                        inv = inv * (D_HEAD ** -0.5)
                    y = y * inv
                qkv_ref[part, r0:r0 + c_len, sl] = y

    if sample:
        gate_ref[0] = jnp.zeros((tb, D_HEAD), F32)
        gate_ref[0, 0:tv, :] = ab_ref[0]
        ab = gate_ref[0]
    else:
        ab = ab_ref[...]
    a_log = par_ref[0, 0:1, :]
    dt_bias = par_ref[0, 1:2, :]
    rowi = lax.broadcasted_iota(jnp.int32, (tb, D_HEAD), 0)
    lane = lax.broadcasted_iota(jnp.int32, (tb, D_HEAD), 1)
    live = rowi < tv
    gdec = -jnp.exp(a_log) * _softplus(ab + dt_bias)
    gate_ref[0] = jnp.where(jnp.logical_and(live, lane < g_heads), gdec, 0.0)
    gate_ref[1] = jnp.where(live, _sigmoid(ab), 0.0)

    ri = lax.broadcasted_iota(jnp.int32, (c_len, c_len), 0)
    ci = lax.broadcasted_iota(jnp.int32, (c_len, c_len), 1)
    incl = ri >= ci
    strict = ri > ci
    eye = jnp.where(ri == ci, 1.0, 0.0).astype(F32)
    tri = jnp.where(ri <= ci, 1.0, 0.0).astype(BF16)
    gn = gn_ref[...]

    def chunk(ck, carry):
        r0 = pl.multiple_of(ck * c_len, c_len)
        rs = pl.ds(r0, c_len)
        g_t = gate_ref[0, rs, :].T
        hi, mid, lo = _split3(g_t)
        gc_t = _dot(hi, tri) + (_dot(mid, tri) + _dot(lo, tri))
        gc = gc_t.T
        beta_all = gate_ref[1, rs, :]
        hs = range(g_heads)
        sls = [slice(g * D_HEAD, (g + 1) * D_HEAD) for g in hs]
        q = [qkv_ref[0, rs, sl] for sl in sls]
        k = [qkv_ref[1, rs, sl] for sl in sls]
        v = [qkv_ref[2, rs, sl] for sl in sls]
        s_old = [s_ref[g] for g in hs]
        if sample:
            zz = [z_ref[0, :, sl] for sl in sls]
        else:
            zz = [z_ref[rs, sl] for sl in sls]
        gcol = [gc[:, g:g + 1] for g in hs]
        grow = [gc_t[g:g + 1, :] for g in hs]
        glast = [gcol[g][c_len - 1:c_len, :] for g in hs]
        beta = [beta_all[:, g_heads + g:g_heads + g + 1] for g in hs]
        decay = [jnp.exp(jnp.where(incl, gcol[g] - grow[g], -jnp.inf)) for g in hs]
        eg = [jnp.exp(gcol[g]) for g in hs]
        kb = [k[g] * beta[g] for g in hs]
        vb = [v[g] * beta[g] for g in hs]
        k16 = [k[g].astype(BF16) for g in hs]
        m = [jnp.where(strict, _dot_nt(kb[g].astype(BF16), k16[g]) * decay[g], 0.0) for g in hs]
        qk = [jnp.where(incl, _dot_nt(q[g].astype(BF16), k16[g]) * decay[g], 0.0) for g in hs]
        tmat = _unit_lower_inverses(m, eye)
        rhs = [jnp.concatenate([vb[g], kb[g] * eg[g]], axis=1).astype(BF16) for g in hs]
        uw = [_dot(tmat[g].astype(BF16), rhs[g]) for g in hs]
        s16 = [s_old[g].astype(BF16) for g in hs]
        ws = [_dot(jnp.concatenate([uw[g][:, D_HEAD:], q[g] * eg[g]], axis=0).astype(BF16), s16[g])
              for g in hs]
        v_new = [uw[g][:, :D_HEAD] - ws[g][:c_len] for g in hs]
        v16 = [v_new[g].astype(BF16) for g in hs]
        o = [ws[g][c_len:] + _dot(qk[g].astype(BF16), v16[g]) for g in hs]
        kg_t = [(k[g] * jnp.exp(glast[g] - gcol[g])).T.astype(BF16) for g in hs]
        s_new = [s_old[g] * jnp.exp(glast[g]) + _dot(kg_t[g], v16[g]) for g in hs]
        out = [_rms(o[g], gn) for g in hs]
        if sample:
            out = [out[g][0:tv] * (zz[g] * _sigmoid(zz[g])) for g in hs]
            o_ref[0] = jnp.concatenate(out, axis=1)
        else:
            out = [out[g] * (zz[g] * _sigmoid(zz[g])) for g in hs]
            o_ref[rs, :] = jnp.concatenate(out, axis=1).astype(o_ref.dtype)
        for g in hs:
            s_ref[g] = s_new[g]
        return carry

    lax.fori_loop(0, tb // c_len, chunk, 0)

    @pl.when(ti == n_tb - 1)
    def _():
        rec_ref[0] = s_ref[...]


def _gdn_scratch(tb):
    g = GDN_HEADS_PER_STEP
    gw = g * D_HEAD
    return [pltpu.VMEM((3, tb + SUBLANES, gw), F32),
            pltpu.VMEM((3, tb, gw), F32),
            pltpu.VMEM((2, tb, D_HEAD), F32),
            pltpu.VMEM((g, D_HEAD, D_HEAD), F32)]


def _gdn_params(a_log, dt_bias, n_heads):
    g = GDN_HEADS_PER_STEP
    ng = n_heads // g
    par = jnp.zeros((ng, SUBLANES, D_HEAD), F32)
    par = par.at[:, 0, :g].set(a_log.reshape(ng, g).astype(F32))
    par = par.at[:, 1, :g].set(dt_bias.reshape(ng, g).astype(F32))
    return par


def _gdn_prompt(qkvz, gates, gate_col, w_conv, par, gn, *, nb, t, n_heads):
    g = GDN_HEADS_PER_STEP
    gw = g * D_HEAD
    ng = n_heads // g
    tb = GDN_PROMPT_ROWS
    n_tb = t // tb
    hd = n_heads * D_HEAD
    m = qkvz.shape[0]

    def cur(part):
        return pl.BlockSpec((tb, gw), lambda b, hg, ti: (b * n_tb + ti, part * ng + hg))

    def prev(part):
        return pl.BlockSpec(
            (SUBLANES, gw),
            lambda b, hg, ti: (jnp.maximum((b * n_tb + ti) * (tb // SUBLANES) - 1, 0), part * ng + hg))

    def wspec(part):
        return pl.BlockSpec((CONV_W, gw), lambda b, hg, ti: (0, part * ng + hg))

    in_specs = [cur(0), cur(1), cur(2), cur(3), prev(0), prev(1), prev(2),
                pl.BlockSpec((tb, D_HEAD), lambda b, hg, ti: (b * n_tb + ti, gate_col + hg)),
                wspec(0), wspec(1), wspec(2),
                pl.BlockSpec((1, SUBLANES, D_HEAD), lambda b, hg, ti: (hg, 0, 0)),
                pl.BlockSpec((1, D_HEAD), lambda b, hg, ti: (0, 0))]
    out_shape = (jax.ShapeDtypeStruct((m, 2 * hd), BF16),
                 jax.ShapeDtypeStruct((nb, n_heads, D_HEAD, D_HEAD), F32))
    out_specs = (pl.BlockSpec((tb, gw), lambda b, hg, ti: (b * n_tb + ti, hg)),
                 pl.BlockSpec((1, g, D_HEAD, D_HEAD), lambda b, hg, ti: (b, hg, 0, 0)))
    return pl.pallas_call(
        functools.partial(_gdn_body, tb=tb, tv=tb, sample=False, n_tb=n_tb),
        out_shape=out_shape,
        grid=(nb, ng, n_tb),
        in_specs=in_specs,
        out_specs=out_specs,
        scratch_shapes=_gdn_scratch(tb),
        compiler_params=_cparams(("parallel", "parallel", "arbitrary")),
        name="gdn_prompt",
    )(qkvz, qkvz, qkvz, qkvz, qkvz, qkvz, qkvz, gates, w_conv, w_conv, w_conv,
      par, gn.reshape(1, D_HEAD))


def _gdn_sample(x, conv_prev, ab, rec_prev, w_conv, par, gn, *, n_heads):
    nb, t, _ = x.shape
    g = GDN_HEADS_PER_STEP
    gw = g * D_HEAD
    ng = n_heads // g
    tb = GDN_CHUNK
    hd = n_heads * D_HEAD
    assert t <= tb

    def cur(part):
        return pl.BlockSpec((1, t, gw), lambda b, hg, ti: (b, 0, part * ng + hg))

    def prev(part):
        return pl.BlockSpec((1, CONV_W - 1, gw), lambda b, hg, ti: (b, 0, part * ng + hg))

    def wspec(part):
        return pl.BlockSpec((CONV_W, gw), lambda b, hg, ti: (0, part * ng + hg))

    in_specs = [cur(0), cur(1), cur(2), cur(3), prev(0), prev(1), prev(2),
                pl.BlockSpec((1, t, D_HEAD), lambda b, hg, ti: (b, 0, hg)),
                wspec(0), wspec(1), wspec(2),
                pl.BlockSpec((1, SUBLANES, D_HEAD), lambda b, hg, ti: (hg, 0, 0)),
                pl.BlockSpec((1, D_HEAD), lambda b, hg, ti: (0, 0)),
                pl.BlockSpec((1, g, D_HEAD, D_HEAD), lambda b, hg, ti: (b, hg, 0, 0))]
    out_shape = (jax.ShapeDtypeStruct((nb, t, hd), F32),
                 jax.ShapeDtypeStruct((nb, n_heads, D_HEAD, D_HEAD), F32))
    out_specs = (pl.BlockSpec((1, t, gw), lambda b, hg, ti: (b, 0, hg)),
                 pl.BlockSpec((1, g, D_HEAD, D_HEAD), lambda b, hg, ti: (b, hg, 0, 0)))
    return pl.pallas_call(
        functools.partial(_gdn_body, tb=tb, tv=t, sample=True, n_tb=1),
        out_shape=out_shape,
        grid=(nb, ng, 1),
        in_specs=in_specs,
        out_specs=out_specs,
        scratch_shapes=_gdn_scratch(tb),
        compiler_params=_cparams(("parallel", "parallel", "arbitrary")),
        name="gdn_sample",
    )(x, x, x, x, conv_prev, conv_prev, conv_prev, ab, w_conv, w_conv, w_conv,
      par, gn.reshape(1, D_HEAD), rec_prev)


def _layer(l, x, h, cache_sb_k, cache_sb_v, state_gdn_conv, state_gdn_rec, cache_mem_k,
           cache_mem_v, page_table, mem_prompt, w_in, w_conv, a_log, dt_bias, gdn_norm,
           sb_norm, sb_bias, w_out, norm_mem, mem_norm, w_mem_q, w_mem_k, w_mem_v, w_mem_o,
           norm_mlp, w_up, w_down, *, bp, tp, bs, ts):
    m, d_model = x.shape
    mp, ms = bp * tp, bs * ts
    n_heads = w_conv.shape[-1] // (3 * D_HEAD)
    hd = n_heads * D_HEAD
    n_sb = sb_bias.shape[-1]
    sbd = n_sb * D_HEAD
    gh = GDN_HEADS_PER_STEP
    ng = n_heads // gh
    tm = MM_TM if m % MM_TM == 0 else ROW_TILE

    w_in_t = jnp.swapaxes(w_in, 1, 2)[l]
    ab_col = 4 * hd
    sb_col = ab_col + 2 * n_heads
    tn_in = MM_TN
    while (4 * hd) % tn_in or (3 * sbd) % tn_in:
        tn_in //= 2
    proj = _matmul_nt(h, w_in_t, [(0, 4 * hd), (sb_col, 3 * sbd)], tm=tm, tn=tn_in, name="in_proj")
    w_a = w_in_t[ab_col:ab_col + n_heads].reshape(ng, gh, d_model)
    w_b = w_in_t[ab_col + n_heads:sb_col].reshape(ng, gh, d_model)
    w_ab = jnp.concatenate([w_a, w_b, jnp.zeros((ng, D_HEAD - 2 * gh, d_model), w_in_t.dtype)],
                           axis=1).reshape(ng * D_HEAD, d_model)
    gates = _matmul_nt(h, w_ab, [(0, ng * D_HEAD)], tm=tm, tn=ng * D_HEAD, name="in_proj_gates")
    sb0 = 4 * n_heads

    par = _gdn_params(a_log[l], dt_bias[l], n_heads)
    mix, rec_p = _gdn_prompt(proj, gates, 0, w_conv[l], par, gdn_norm[l],
                             nb=bp, t=tp, n_heads=n_heads)
    samp = proj[mp:]
    xs_rows = samp[:, :4 * hd].reshape(bs, ts, 4 * hd)
    ab_s = gates[mp:].reshape(bs, ts, ng * D_HEAD)
    og_s, rec_s = _gdn_sample(xs_rows, state_gdn_conv[l], ab_s, state_gdn_rec[l], w_conv[l], par,
                              gdn_norm[l], n_heads=n_heads)
    conv_p = jnp.stack([lax.slice(proj, ((b + 1) * tp - (CONV_W - 1), 0), ((b + 1) * tp, 3 * hd))
                        for b in range(bp)])
    conv_s = jnp.concatenate([state_gdn_conv[l], xs_rows[:, :, :3 * hd]], axis=1)[:, -(CONV_W - 1):]

    mix = _sb_prompt(proj, mix, sb_bias[l].astype(F32), sb_norm[l], nb=bp, t=tp, n_heads=n_sb,
                     q_col=sb0, k_col=sb0 + n_sb, v_col=sb0 + 2 * n_sb, o_col=n_heads)
    q_s = samp[:, 4 * hd:4 * hd + sbd].reshape(bs, ts, sbd)
    k_s = samp[:, 4 * hd + sbd:4 * hd + 2 * sbd].reshape(bs, ts, sbd)
    v_s = samp[:, 4 * hd + 2 * sbd:].reshape(bs, ts, sbd)
    n_phys = cache_sb_k.shape[1]
    ck = cache_sb_k[l].reshape(n_phys, PAGE_SIZE * n_sb, D_HEAD)
    cv = cache_sb_v[l].reshape(n_phys, PAGE_SIZE * n_sb, D_HEAD)
    os_s = _sb_sample(q_s, k_s, v_s, ck, cv, page_table, sb_bias[l], sb_norm[l])
    mix_s = jnp.concatenate([og_s.reshape(ms, hd), os_s.reshape(ms, sbd)], axis=1).astype(BF16)
    mix = lax.dynamic_update_slice(mix, mix_s, (mp, 0))
    k_p = proj[:mp, 4 * hd + sbd:4 * hd + 2 * sbd].reshape(bp, tp, n_sb, D_HEAD)
    v_p = proj[:mp, 4 * hd + 2 * sbd:4 * hd + 3 * sbd].reshape(bp, tp, n_sb, D_HEAD)

    x = _matmul(mix, w_out[l], d_model, tm=tm, res=x, name="out_proj")

    n_mem = mem_prompt.shape[1]
    md = w_mem_k.shape[-1]
    memn = _rmsnorm(mem_prompt.reshape(bp * n_mem, d_model), mem_norm[l], BF16, n_mem)
    mk = _matmul(memn, w_mem_k[l], md, tm=bp * n_mem, name="mem_k_proj")
    mv = _matmul(memn, w_mem_v[l], md, tm=bp * n_mem, name="mem_v_proj")
    qm = _norm_matmul(x, norm_mem[l], w_mem_q[l], tm=tm, out_dtype=BF16)
    om_p = _mem_attn(qm[:mp].reshape(bp, tp, md), mk.reshape(bp, n_mem, md), mv.reshape(bp, n_mem, md),
                     min(tp, 512), BF16)
    qm_s = jnp.pad(qm[mp:].astype(F32).reshape(bs, ts, md), ((0, 0), (0, SUBLANES - ts), (0, 0)))
    om_s = _mem_attn(qm_s, cache_mem_k[l].reshape(bs, n_mem, md), cache_mem_v[l].reshape(bs, n_mem, md),
                     SUBLANES, F32)[:, :ts]
    om = jnp.concatenate([om_p.reshape(mp, md), om_s.reshape(ms, md).astype(BF16)], axis=0)
    x, h = _matmul_res_norm(om, w_mem_o[l], x, norm_mlp[l], tm=tm // 2 if tm % 32 == 0 else tm)

    d_ff = w_up.shape[-1]
    hid = _matmul(h, w_up[l], d_ff, tm=tm, relu2=True, out_dtype=BF16, name="mlp_up")
    half = d_ff // 2
    x = _matmul(hid, w_down[l], d_model, tm=tm, k=half, res=x, w_single_buffer=True,
                name="mlp_down_lo")
    x = _matmul(hid, w_down[l], d_model, tm=tm, k=half, a_col_block=1, w_row_block=1, res=x,
                w_single_buffer=True, name="mlp_down_hi")

    n_mh = md // D_HEAD
    outs = dict(k_p=k_p, v_p=v_p, k_s=k_s.reshape(bs, ts, n_sb, D_HEAD),
                v_s=v_s.reshape(bs, ts, n_sb, D_HEAD), conv_p=conv_p, conv_s=conv_s,
                rec_p=rec_p, rec_s=rec_s,
                mk=mk.reshape(bp, n_mem, n_mh, D_HEAD), mv=mv.reshape(bp, n_mem, n_mh, D_HEAD))
    return x, outs


def kernel(x_prompt, x_sample, cache_sb_k, cache_sb_v, state_gdn_conv, state_gdn_rec, cache_mem_k,
           cache_mem_v, page_table, mem_prompt, norm_mix, w_in, w_conv, a_log, dt_bias, gdn_norm,
           sb_norm, sb_bias, w_out, norm_mem, mem_norm, w_mem_q, w_mem_k, w_mem_v, w_mem_o,
           norm_mlp, w_up, w_down, norm_final):
    bp, tp, d_model = x_prompt.shape
    bs, ts, _ = x_sample.shape
    mp, ms = bp * tp, bs * ts
    depth = w_in.shape[0]
    h, x = _rmsnorm_join(x_prompt.reshape(mp, d_model), x_sample.reshape(ms, d_model), norm_mix[0])
    per_layer = []
    for l in range(depth):
        if l > 0:
            h = _rmsnorm(x, norm_mix[l], BF16, ROW_TILE)
        x, outs = _layer(
            l, x, h, cache_sb_k, cache_sb_v, state_gdn_conv, state_gdn_rec, cache_mem_k,
            cache_mem_v, page_table, mem_prompt, w_in, w_conv, a_log, dt_bias, gdn_norm,
            sb_norm, sb_bias, w_out, norm_mem, mem_norm, w_mem_q, w_mem_k, w_mem_v, w_mem_o,
            norm_mlp, w_up, w_down, bp=bp, tp=tp, bs=bs, ts=ts)
        per_layer.append(outs)
    tile_p = 4 * ROW_TILE if mp % (4 * ROW_TILE) == 0 else ROW_TILE
    y_p = _rmsnorm(x, norm_final, F32, tile_p, rows=mp).reshape(bp, tp, d_model)
    y_s = _rmsnorm(x, norm_final, F32, ms, row_block_offset=mp // ms, rows=ms).reshape(bs, ts, d_model)

    def stack(name):
        return jnp.stack([o[name] for o in per_layer])

    return (y_p, y_s, stack("k_p"), stack("v_p"), stack("k_s"), stack("v_s"),
            stack("conv_p"), stack("conv_s"), stack("rec_p"), stack("rec_s"),
            stack("mk"), stack("mv"))
```

```python
import functools
import math

import jax
import jax.numpy as jnp
from jax import lax
from jax.experimental import pallas as pl
from jax.experimental.pallas import tpu as pltpu

F32 = jnp.float32
BF16 = jnp.bfloat16

EPS = 1e-6
D_HEAD = 128
SUBLANES = 8
GDN_CHUNK = 64
GDN_HEADS_PER_STEP = 16
GDN_PROMPT_ROWS = 256
CONV_W = 4
PAGE_SIZE = 128
SB_TQ = 512
SB_TK = 128
SB_PAGES_PER_STEP = 8
V7X_VMEM_LIMIT_BYTES = 56 * 1024 * 1024
MM_TN = 512
MM_TN_WIDE = 1024
MM_TM = 640
ROW_TILE = 128


def _cparams(sem):
    return pltpu.CompilerParams(dimension_semantics=sem, vmem_limit_bytes=V7X_VMEM_LIMIT_BYTES)


def _softplus(x):
    return jnp.maximum(x, 0.0) + jnp.log1p(jnp.exp(-jnp.abs(x)))


def _sigmoid(x):
    return 1.0 / (1.0 + jnp.exp(-x))


def _dot(a, b):
    return jnp.dot(a, b, preferred_element_type=F32)


def _dot_nt(a, b):
    return lax.dot_general(a, b, (((1,), (1,)), ((), ())), preferred_element_type=F32)


def _split2(x):
    hi = x.astype(BF16)
    lo = (x - hi.astype(F32)).astype(BF16)
    return hi, lo


def _split3(x):
    hi = x.astype(BF16)
    r = x - hi.astype(F32)
    mid = r.astype(BF16)
    lo = (r - mid.astype(F32)).astype(BF16)
    return hi, mid, lo


def _dot_hp(a, b):
    a_hi, a_lo = _split2(a)
    b_hi, b_lo = _split2(b)
    return _dot(a_hi, b_hi) + (_dot(a_hi, b_lo) + _dot(a_lo, b_hi))


def _rms(x, g):
    y = x * lax.rsqrt(jnp.mean(x * x, axis=-1, keepdims=True) + EPS)
    return y * g


def _rmsnorm_body(x_ref, g_ref, o_ref):
    o_ref[...] = _rms(x_ref[...], g_ref[...]).astype(o_ref.dtype)


def _rmsnorm(x, g, out_dtype, tm, row_block_offset=0, rows=None):
    m, d = x.shape
    rows = m if rows is None else rows
    off = row_block_offset
    assert rows % tm == 0
    return pl.pallas_call(
        _rmsnorm_body,
        out_shape=jax.ShapeDtypeStruct((rows, d), out_dtype),
        grid=(rows // tm,),
        in_specs=[pl.BlockSpec((tm, d), lambda i: (i + off, 0)),
                  pl.BlockSpec((1, d), lambda i: (0, 0))],
        out_specs=pl.BlockSpec((tm, d), lambda i: (i, 0)),
        compiler_params=_cparams(("parallel",)),
        name="rmsnorm",
    )(x, g.reshape(1, d))


def _rmsnorm_join_body(xp_ref, xs_ref, g_ref, h_ref, x_ref, *, n_prompt_blocks):
    i = pl.program_id(0)

    @pl.when(i < n_prompt_blocks)
    def _():
        x = xp_ref[...]
        x_ref[...] = x
        h_ref[...] = _rms(x, g_ref[...]).astype(h_ref.dtype)

    @pl.when(i >= n_prompt_blocks)
    def _():
        x = xs_ref[...]
        x_ref[...] = x
        h_ref[...] = _rms(x, g_ref[...]).astype(h_ref.dtype)


def _rmsnorm_join(xp, xs, g):
    mp, d = xp.shape
    ms = xs.shape[0]
    tm = ms
    assert mp % tm == 0
    npb = mp // tm
    return pl.pallas_call(
        functools.partial(_rmsnorm_join_body, n_prompt_blocks=npb),
        out_shape=(jax.ShapeDtypeStruct((mp + ms, d), BF16),
                   jax.ShapeDtypeStruct((mp + ms, d), F32)),
        grid=(npb + 1,),
        in_specs=[pl.BlockSpec((tm, d), lambda i: (jnp.minimum(i, npb - 1), 0)),
                  pl.BlockSpec((tm, d), lambda i: (0, 0)),
                  pl.BlockSpec((1, d), lambda i: (0, 0))],
        out_specs=(pl.BlockSpec((tm, d), lambda i: (i, 0)),
                   pl.BlockSpec((tm, d), lambda i: (i, 0))),
        compiler_params=_cparams(("arbitrary",)),
        name="rmsnorm_join",
    )(xp, xs, g.reshape(1, d))


def _matmul_body(*refs, relu2, has_res):
    if has_res:
        a_ref, w_ref, r_ref, o_ref, wbf_ref = refs
    else:
        a_ref, w_ref, o_ref, wbf_ref = refs

    @pl.when(pl.program_id(1) == 0)
    def _():
        wbf_ref[...] = w_ref[...].astype(BF16)

    acc = _dot(a_ref[...], wbf_ref[...])
    if relu2:
        acc = jnp.square(jnp.maximum(acc, 0.0))
    if has_res:
        acc = acc + r_ref[...]
    o_ref[...] = acc.astype(o_ref.dtype)


def _matmul(a, w, n, *, tm, k=None, tn=MM_TN, a_col_block=0, w_row_block=0, w_col_block=0,
            res=None, relu2=False, out_dtype=F32, w_single_buffer=False, name="matmul"):
    m = a.shape[0]
    k = a.shape[1] if k is None else k
    while n % tn:
        tn //= 2
    assert m % tm == 0 and tn % D_HEAD == 0
    w_mode = dict(pipeline_mode=pl.Buffered(1)) if w_single_buffer else {}
    in_specs = [pl.BlockSpec((tm, k), lambda j, i: (i, a_col_block)),
                pl.BlockSpec((k, tn), lambda j, i: (w_row_block, j + w_col_block), **w_mode)]
    args = [a, w]
    if res is not None:
        in_specs.append(pl.BlockSpec((tm, tn), lambda j, i: (i, j)))
        args.append(res)
    return pl.pallas_call(
        functools.partial(_matmul_body, relu2=relu2, has_res=res is not None),
        out_shape=jax.ShapeDtypeStruct((m, n), out_dtype),
        grid=(n // tn, m // tm),
        in_specs=in_specs,
        out_specs=pl.BlockSpec((tm, tn), lambda j, i: (i, j)),
        scratch_shapes=[pltpu.VMEM((k, tn), BF16)],
        compiler_params=_cparams(("parallel", "arbitrary")),
        name=name,
    )(*args)


def _norm_matmul_body(x_ref, g_ref, w_ref, o_ref, wbf_ref):
    @pl.when(pl.program_id(0) == 0)
    def _():
        wbf_ref[...] = w_ref[...].astype(BF16)

    h = _rms(x_ref[...], g_ref[...]).astype(BF16)
    o_ref[...] = _dot(h, wbf_ref[...]).astype(o_ref.dtype)


def _norm_matmul(x, g, w, *, tm, out_dtype):
    m, k = x.shape
    n = w.shape[1]
    assert m % tm == 0
    return pl.pallas_call(
        _norm_matmul_body,
        out_shape=jax.ShapeDtypeStruct((m, n), out_dtype),
        grid=(m // tm,),
        in_specs=[pl.BlockSpec((tm, k), lambda i: (i, 0)),
                  pl.BlockSpec((1, k), lambda i: (0, 0)),
                  pl.BlockSpec((k, n), lambda i: (0, 0))],
        out_specs=pl.BlockSpec((tm, n), lambda i: (i, 0)),
        scratch_shapes=[pltpu.VMEM((k, n), BF16)],
        compiler_params=_cparams(("arbitrary",)),
        name="norm_matmul",
    )(x, g.reshape(1, k), w)


def _matmul_res_norm_body(a_ref, w_ref, r_ref, g_ref, x_ref, h_ref, wbf_ref):
    @pl.when(pl.program_id(0) == 0)
    def _():
        wbf_ref[...] = w_ref[...].astype(BF16)

    x = _dot(a_ref[...], wbf_ref[...]) + r_ref[...]
    x_ref[...] = x
    h_ref[...] = _rms(x, g_ref[...]).astype(h_ref.dtype)


def _matmul_res_norm(a, w, res, g, *, tm):
    m, k = a.shape
    n = w.shape[1]
    assert m % tm == 0
    return pl.pallas_call(
        _matmul_res_norm_body,
        out_shape=(jax.ShapeDtypeStruct((m, n), F32), jax.ShapeDtypeStruct((m, n), BF16)),
        grid=(m // tm,),
        in_specs=[pl.BlockSpec((tm, k), lambda i: (i, 0)),
                  pl.BlockSpec((k, n), lambda i: (0, 0)),
                  pl.BlockSpec((tm, n), lambda i: (i, 0)),
                  pl.BlockSpec((1, n), lambda i: (0, 0))],
        out_specs=(pl.BlockSpec((tm, n), lambda i: (i, 0)),
                   pl.BlockSpec((tm, n), lambda i: (i, 0))),
        scratch_shapes=[pltpu.VMEM((k, n), BF16)],
        compiler_params=_cparams(("arbitrary",)),
        name="matmul_res_norm",
    )(a, w, res, g.reshape(1, n))


def _matmul_nt_body(a_ref, w_ref, o_ref, wbf_ref):
    @pl.when(pl.program_id(1) == 0)
    def _():
        wbf_ref[...] = w_ref[...].astype(BF16)

    o_ref[...] = _dot_nt(a_ref[...], wbf_ref[...]).astype(o_ref.dtype)


def _matmul_nt(a, wt, row_segments, *, tm, tn=MM_TN, w_single_buffer=False, name="matmul_nt"):
    m, k = a.shape
    w_mode = dict(pipeline_mode=pl.Buffered(1)) if w_single_buffer else {}
    starts = []
    for first, n_rows in row_segments:
        assert first % SUBLANES == 0 and n_rows % tn == 0
        starts += [first + i * tn for i in range(n_rows // tn)]
    n = tn * len(starts)
    def w_row(j):
        row8 = j * (tn // SUBLANES)
        base, shift = 0, 0
        for first, n_rows in row_segments:
            step = (first - base * tn) - shift
            if step:
                row8 = row8 + jnp.where(j >= base, step // SUBLANES, 0)
            shift += step
            base += n_rows // tn
        return row8 * SUBLANES

    return pl.pallas_call(
        _matmul_nt_body,
        out_shape=jax.ShapeDtypeStruct((m, n), F32),
        grid=(n // tn, m // tm),
        in_specs=[pl.BlockSpec((tm, k), lambda j, i: (i, 0)),
                  pl.BlockSpec((pl.Element(tn), pl.Element(k)), lambda j, i: (w_row(j), 0), **w_mode)],
        out_specs=pl.BlockSpec((tm, tn), lambda j, i: (i, j)),
        scratch_shapes=[pltpu.VMEM((tn, k), BF16)],
        compiler_params=_cparams(("parallel", "arbitrary")),
        name=name,
    )(a, wt)


def _mem_attn_body(q_ref, k_ref, v_ref, o_ref, *, n_heads, scale):
    for h in range(n_heads):
        sl = slice(h * D_HEAD, (h + 1) * D_HEAD)
        q = q_ref[0, :, sl].astype(BF16)
        k = k_ref[0, :, sl].astype(BF16)
        v = v_ref[0, :, sl].astype(BF16)
        s = _dot_nt(q, k) * scale
        s = s - jnp.max(s, axis=-1, keepdims=True)
        e = jnp.exp(s)
        p = e / jnp.sum(e, axis=-1, keepdims=True)
        o_ref[0, :, sl] = _dot(p.astype(BF16), v).astype(o_ref.dtype)


def _mem_attn(q, k, v, tq, out_dtype):
    nb, t, hd = q.shape
    n_mem = k.shape[1]
    return pl.pallas_call(
        functools.partial(_mem_attn_body, n_heads=hd // D_HEAD, scale=D_HEAD ** -0.5),
        out_shape=jax.ShapeDtypeStruct((nb, t, hd), out_dtype),
        grid=(nb, t // tq),
        in_specs=[pl.BlockSpec((1, tq, hd), lambda b, i: (b, i, 0)),
                  pl.BlockSpec((1, n_mem, hd), lambda b, i: (b, 0, 0)),
                  pl.BlockSpec((1, n_mem, hd), lambda b, i: (b, 0, 0))],
        out_specs=pl.BlockSpec((1, tq, hd), lambda b, i: (b, i, 0)),
        compiler_params=_cparams(("parallel", "parallel")),
        name="mem_attn",
    )(q, k, v)


def _later_keys_neg(n):
    r = lax.broadcasted_iota(jnp.int32, (n, n), 0)
    c = lax.broadcasted_iota(jnp.int32, (n, n), 1)
    return jnp.where(r > c, -1.0, 0.0).astype(BF16)


LOG2_E = 1.4426950408889634


def _sb_blocks(zs, valids, c, u_neg, lead_cols=1):
    sps = [jnp.maximum(z, 0.0) + jnp.log2(1.0 + jnp.exp2(-jnp.abs(z))) for z in zs]
    stays = [sp if v is None else jnp.where(v, sp, 0.0) for sp, v in zip(sps, valids)]
    parts = [_split2(st) for st in stays]
    afters = [_dot(hi, u_neg) + _dot(lo, u_neg) for hi, lo in parts]
    out = []
    for z, sp, st, after, v in zip(zs, sps, stays, afters, valids):
        a = jnp.exp2((z - sp) + after + c)
        out.append(a if v is None else jnp.where(v, a, 0.0))
        if lead_cols:
            for j in range(lead_cols):
                c = c + (after[:, j:j + 1] - st[:, j:j + 1])
        else:
            c = c - jnp.sum(st, axis=-1, keepdims=True)
    return out, c


def _sb_prompt_body(bias_ref, q_ref, k_ref, v_ref, g_ref, mix_ref, o_ref, *, tq, tk, scale):
    del mix_ref
    h = pl.program_id(1)
    qi = pl.program_id(2)
    r = tq // tk
    bias = bias_ref[h] * LOG2_E
    scale = scale * LOG2_E
    q = q_ref[...].astype(BF16)
    u = _later_keys_neg(tk)

    def blocks(qrows, row0, j_hi, masks, c, acc):
        n = qrows.shape[0]
        k0s = [pl.multiple_of((j_hi - i) * tk, tk) for i in range(len(masks))]
        zs = [_dot_nt(qrows, k_ref[pl.ds(k0, tk), :].astype(BF16)) * scale + bias for k0 in k0s]
        row = lax.broadcasted_iota(jnp.int32, (n, tk), 0) + (qi * tq + row0)
        col = lax.broadcasted_iota(jnp.int32, (n, tk), 1)
        valids = [((k0 + col) < row) if m else None for k0, m in zip(k0s, masks)]
        ws, c = _sb_blocks(zs, valids, c, u)
        for w, k0 in zip(ws, k0s):
            acc = acc + _dot(w.astype(BF16), v_ref[pl.ds(k0, tk), :].astype(BF16))
        return c, acc

    hr = r // 2
    half = tq // 2
    zc = jnp.zeros((half, 1), F32)
    za = jnp.zeros((half, D_HEAD), F32)
    c_top, acc_top = blocks(q[:half], 0, qi * r + hr - 1, [True] * hr, zc, za)
    c_bot, acc_bot = blocks(q[half:], half, qi * r + r - 1, [True] * hr + [False] * hr, zc, za)
    c = jnp.concatenate([c_top, c_bot], axis=0)
    acc = jnp.concatenate([acc_top, acc_bot], axis=0)

    def body(it, carry):
        return blocks(q, 0, (qi - 1 - it) * r + (r - 1), [False] * r, *carry)

    c, acc = lax.fori_loop(0, qi, body, (c, acc))
    o_ref[...] = _rms(acc, g_ref[...]).astype(o_ref.dtype)


def _sb_prompt(qkv, mix, bias, gain, *, nb, t, n_heads, q_col, k_col, v_col, o_col):
    tq, tk = SB_TQ, SB_TK
    nq = t // tq
    return pl.pallas_call(
        functools.partial(_sb_prompt_body, tq=tq, tk=tk, scale=D_HEAD ** -0.5),
        out_shape=jax.ShapeDtypeStruct(mix.shape, mix.dtype),
        grid=(nb, n_heads, nq),
        in_specs=[pl.BlockSpec(memory_space=pltpu.SMEM),
                  pl.BlockSpec((tq, D_HEAD), lambda b, h, i: (b * nq + i, q_col + h)),
                  pl.BlockSpec((t, D_HEAD), lambda b, h, i: (b, k_col + h)),
                  pl.BlockSpec((t, D_HEAD), lambda b, h, i: (b, v_col + h)),
                  pl.BlockSpec((1, D_HEAD), lambda b, h, i: (0, 0)),
                  pl.BlockSpec(memory_space=pl.ANY)],
        out_specs=pl.BlockSpec((tq, D_HEAD), lambda b, h, i: (b * nq + i, o_col + h)),
        input_output_aliases={5: 0},
        compiler_params=_cparams(("parallel", "parallel", "arbitrary")),
        name="sb_prompt",
    )(bias, qkv, qkv, qkv, gain.reshape(1, D_HEAD), mix)


def _sb_sample_body(pt_ref, q_ref, kn_ref, vn_ref, *rest, t, n_heads, n_steps, pages_per_step, scale):
    del pt_ref
    pps = pages_per_step
    ck_refs = rest[:pps]
    cv_refs = rest[pps:2 * pps]
    bias_ref, g_ref, o_ref, qall_ref, c_ref, acc_ref, new_ref, kv16_ref = rest[2 * pps:]
    s = pl.program_id(1)
    n_pairs = n_heads // 2
    rp = 2 * SUBLANES
    rows = n_pairs * rp
    tk = PAGE_SIZE
    cols = 2 * tk
    rr = lax.broadcasted_iota(jnp.int32, (cols, cols), 0)
    cc = lax.broadcasted_iota(jnp.int32, (cols, cols), 1)
    u2 = jnp.where(jnp.logical_and((rr >> 1) > (cc >> 1), (rr & 1) == (cc & 1)), -1.0, 0.0).astype(BF16)
    qrow = lax.broadcasted_iota(jnp.int32, (rows, cols), 0)
    col = lax.broadcasted_iota(jnp.int32, (rows, cols), 1)
    own = ((qrow >> 3) & 1) == (col & 1)

    def process(get_ks, get_vs, valids):
        qb = [qall_ref[g * rp:(g + 1) * rp, :].astype(BF16) for g in range(n_pairs)]
        bias2 = bias_ref[...] * LOG2_E
        zs = [jnp.concatenate([_dot_nt(qb[g], get_k(g)) for g in range(n_pairs)], axis=0)
              * (scale * LOG2_E) + bias2 for get_k in get_ks]
        ws, c_new = _sb_blocks(zs, valids, c_ref[...], u2, lead_cols=0)
        ws = [w.astype(BF16) for w in ws]
        outs = []
        for g in range(n_pairs):
            sl = slice(g * rp, (g + 1) * rp)
            o = acc_ref[sl, :]
            for w, get_v in zip(ws, get_vs):
                o = o + _dot(w[sl, :], get_v(g))
            outs.append(o)
        c_ref[...] = c_new
        acc_ref[...] = jnp.concatenate(outs, axis=0)

    @pl.when(s == 0)
    def _():
        qall_ref[...] = jnp.zeros(qall_ref.shape, F32)
        for hd in range(n_heads):
            qall_ref[hd * SUBLANES:hd * SUBLANES + t, :] = q_ref[0, :, hd * D_HEAD:(hd + 1) * D_HEAD]
        c_ref[...] = jnp.zeros(c_ref.shape, F32)
        acc_ref[...] = jnp.zeros(acc_ref.shape, F32)
        new_ref[...] = jnp.zeros(new_ref.shape, F32)
        for hd in range(n_heads):
            for j in range(t):
                r = 2 * j + (hd & 1)
                new_ref[0, hd >> 1, r:r + 1, :] = kn_ref[0, j:j + 1, hd * D_HEAD:(hd + 1) * D_HEAD]
                new_ref[1, hd >> 1, r:r + 1, :] = vn_ref[0, j:j + 1, hd * D_HEAD:(hd + 1) * D_HEAD]
        qi = qrow & (SUBLANES - 1)
        valid = jnp.logical_and(own, jnp.logical_and((col >> 1) < qi, qi < t))
        process([lambda g: new_ref[0, g].astype(BF16)], [lambda g: new_ref[1, g].astype(BF16)], [valid])

    @pl.when(s > 0)
    def _():
        def pair(slot, ref):
            kv16_ref[slot] = pltpu.bitcast(ref[0].astype(BF16), jnp.uint32)

            def get(g):
                return pltpu.bitcast(kv16_ref[slot, pl.ds(g, tk, stride=n_pairs), :], BF16)
            return get

        process([pair(p, r) for p, r in enumerate(ck_refs)],
                [pair(pps + p, r) for p, r in enumerate(cv_refs)], [own] * pps)

    @pl.when(s == n_steps - 1)
    def _():
        for hd in range(n_heads):
            o = acc_ref[hd * SUBLANES:hd * SUBLANES + t, :]
            o_ref[0, :, hd * D_HEAD:(hd + 1) * D_HEAD] = _rms(o, g_ref[...])


def _sb_sample(q, k_new, v_new, cache_k, cache_v, page_table, bias, gain):
    nb, t, hd = q.shape
    n_heads = hd // D_HEAD
    n_pages = page_table.shape[1]
    assert t <= SUBLANES and n_heads % 2 == 0
    rows = n_heads * SUBLANES
    bias_rows = jnp.repeat(bias.astype(F32), SUBLANES).reshape(rows, 1)

    pps = SB_PAGES_PER_STEP
    while n_pages % pps:
        pps //= 2
    n_steps = n_pages // pps + 1

    def page_spec(p):
        def idx(b, s, pt):
            return (pt[b, n_pages - 1 - ((jnp.maximum(s, 1) - 1) * pps + p)], 0, 0)
        return pl.BlockSpec((1, PAGE_SIZE * n_heads, D_HEAD), idx)

    tok_spec = pl.BlockSpec((1, t, hd), lambda b, s, pt: (b, 0, 0))
    grid_spec = pltpu.PrefetchScalarGridSpec(
        num_scalar_prefetch=1,
        grid=(nb, n_steps),
        in_specs=[tok_spec, tok_spec, tok_spec]
                 + [page_spec(p) for p in range(pps)] + [page_spec(p) for p in range(pps)]
                 + [pl.BlockSpec((rows, 1), lambda b, s, pt: (0, 0)),
                    pl.BlockSpec((1, D_HEAD), lambda b, s, pt: (0, 0))],
        out_specs=pl.BlockSpec((1, t, hd), lambda b, s, pt: (b, 0, 0)),
        scratch_shapes=[pltpu.VMEM((rows, D_HEAD), F32),
                        pltpu.VMEM((rows, 1), F32),
                        pltpu.VMEM((rows, D_HEAD), F32),
                        pltpu.VMEM((2, n_heads // 2, 2 * PAGE_SIZE, D_HEAD), F32),
                        pltpu.VMEM((2 * pps, PAGE_SIZE * n_heads // 2, D_HEAD), jnp.uint32)],
    )
    return pl.pallas_call(
        functools.partial(_sb_sample_body, t=t, n_heads=n_heads, n_steps=n_steps,
                          pages_per_step=pps, scale=D_HEAD ** -0.5),
        out_shape=jax.ShapeDtypeStruct((nb, t, hd), F32),
        grid_spec=grid_spec,
        compiler_params=_cparams(("parallel", "arbitrary")),
        name="sb_sample",
    )(page_table, q, k_new, v_new, *([cache_k] * pps), *([cache_v] * pps), bias_rows,
      gain.reshape(1, D_HEAD))


def _unit_lower_inverses(ms, eye):
    ps = [-m for m in ms]
    ts = [eye + p for p in ps]
    for rnd in range(int(math.log2(ms[0].shape[0])) - 1):
        if rnd == 0:
            ps = [_dot_hp(p, p) for p in ps]
            ts = [t + _dot_hp(t, p) for t, p in zip(ts, ps)]
        else:
            p16 = [p.astype(BF16) for p in ps]
            ps = [_dot(p, p) for p in p16]
            ts = [t + _dot(t.astype(BF16), p.astype(BF16)) for t, p in zip(ts, ps)]
    return ts


def _gdn_body(*refs, tb, tv, sample, n_tb):
    g_heads = GDN_HEADS_PER_STEP
    gw = g_heads * D_HEAD
    c_len = GDN_CHUNK
    if sample:
        (xq_ref, xk_ref, xv_ref, z_ref, pq_ref, pk_ref, pv_ref, ab_ref, wq_ref, wk_ref, wv_ref,
         par_ref, gn_ref, s0_ref, o_ref, rec_ref, ext_ref, qkv_ref, gate_ref, s_ref) = refs
    else:
        (xq_ref, xk_ref, xv_ref, z_ref, pq_ref, pk_ref, pv_ref, ab_ref, wq_ref, wk_ref, wv_ref,
         par_ref, gn_ref, o_ref, rec_ref, ext_ref, qkv_ref, gate_ref, s_ref) = refs
    ti = pl.program_id(2)

    @pl.when(ti == 0)
    def _():
        if sample:
            s_ref[...] = s0_ref[0]
        else:
            s_ref[...] = jnp.zeros(s_ref.shape, F32)

    for part, (x_ref, p_ref, w_ref) in enumerate(((xq_ref, pq_ref, wq_ref),
                                                   (xk_ref, pk_ref, wk_ref),
                                                   (xv_ref, pv_ref, wv_ref))):
        if sample:
            ext_ref[part, 0:SUBLANES, :] = jnp.zeros((SUBLANES, gw), F32)
            ext_ref[part, SUBLANES - (CONV_W - 1):SUBLANES, :] = p_ref[0]
            ext_ref[part, SUBLANES:SUBLANES + tb, :] = jnp.zeros((tb, gw), F32)
            ext_ref[part, SUBLANES:SUBLANES + tv, :] = x_ref[0]
        else:
            ext_ref[part, 0:SUBLANES, :] = jnp.where(ti > 0, p_ref[...], 0.0)
            ext_ref[part, SUBLANES:SUBLANES + tb, :] = x_ref[...]
        base = SUBLANES - (CONV_W - 1)
        for g in range(g_heads):
            sl = slice(g * D_HEAD, (g + 1) * D_HEAD)
            w = w_ref[:, sl]
            for r0 in range(0, tb, c_len):
                y = ext_ref[part, base + r0:base + r0 + c_len, sl] * w[0:1, :]
                for i in range(1, CONV_W):
                    y = y + ext_ref[part, base + r0 + i:base + r0 + i + c_len, sl] * w[i:i + 1, :]
                y = y / (1.0 + jnp.exp2(y * (-LOG2_E)))
                if part < 2:
                    inv = lax.rsqrt(jnp.sum(y * y, axis=-1, keepdims=True) + EPS)
                    if part == 0:
                        inv = inv * (D_HEAD ** -0.5)
                    y = y * inv
                qkv_ref[part, r0:r0 + c_len, sl] = y

    if sample:
        gate_ref[0] = jnp.zeros((tb, D_HEAD), F32)
        gate_ref[0, 0:tv, :] = ab_ref[0]
        ab = gate_ref[0]
    else:
        ab = ab_ref[...]
    a_log = par_ref[0, 0:1, :]
    dt_bias = par_ref[0, 1:2, :]
    rowi = lax.broadcasted_iota(jnp.int32, (tb, D_HEAD), 0)
    lane = lax.broadcasted_iota(jnp.int32, (tb, D_HEAD), 1)
    live = rowi < tv
    gdec = -jnp.exp(a_log) * _softplus(ab + dt_bias)
    gate_ref[0] = jnp.where(jnp.logical_and(live, lane < g_heads), gdec, 0.0)
    gate_ref[1] = jnp.where(live, _sigmoid(ab), 0.0)

    ri = lax.broadcasted_iota(jnp.int32, (c_len, c_len), 0)
    ci = lax.broadcasted_iota(jnp.int32, (c_len, c_len), 1)
    incl = ri >= ci
    strict = ri > ci
    eye = jnp.where(ri == ci, 1.0, 0.0).astype(F32)
    tri = jnp.where(ri <= ci, 1.0, 0.0).astype(BF16)
    gn = gn_ref[...]

    def chunk(ck, carry):
        r0 = pl.multiple_of(ck * c_len, c_len)
        rs = pl.ds(r0, c_len)
        g_t = gate_ref[0, rs, :].T
        hi, mid, lo = _split3(g_t)
        gc_t = _dot(hi, tri) + (_dot(mid, tri) + _dot(lo, tri))
        gc = gc_t.T
        beta_all = gate_ref[1, rs, :]
        hs = range(g_heads)
        sls = [slice(g * D_HEAD, (g + 1) * D_HEAD) for g in hs]
        q = [qkv_ref[0, rs, sl] for sl in sls]
        k = [qkv_ref[1, rs, sl] for sl in sls]
        v = [qkv_ref[2, rs, sl] for sl in sls]
        s_old = [s_ref[g] for g in hs]
        if sample:
            zz = [z_ref[0, :, sl] for sl in sls]
        else:
            zz = [z_ref[rs, sl] for sl in sls]
        gcol = [gc[:, g:g + 1] for g in hs]
        grow = [gc_t[g:g + 1, :] for g in hs]
        glast = [gcol[g][c_len - 1:c_len, :] for g in hs]
        beta = [beta_all[:, g_heads + g:g_heads + g + 1] for g in hs]
        decay = [jnp.exp(jnp.where(incl, gcol[g] - grow[g], -jnp.inf)) for g in hs]
        eg = [jnp.exp(gcol[g]) for g in hs]
        kb = [k[g] * beta[g] for g in hs]
        vb = [v[g] * beta[g] for g in hs]
        k16 = [k[g].astype(BF16) for g in hs]
        m = [jnp.where(strict, _dot_nt(kb[g].astype(BF16), k16[g]) * decay[g], 0.0) for g in hs]
        qk = [jnp.where(incl, _dot_nt(q[g].astype(BF16), k16[g]) * decay[g], 0.0) for g in hs]
        tmat = _unit_lower_inverses(m, eye)
        rhs = [jnp.concatenate([vb[g], kb[g] * eg[g]], axis=1).astype(BF16) for g in hs]
        uw = [_dot(tmat[g].astype(BF16), rhs[g]) for g in hs]
        s16 = [s_old[g].astype(BF16) for g in hs]
        ws = [_dot(jnp.concatenate([uw[g][:, D_HEAD:], q[g] * eg[g]], axis=0).astype(BF16), s16[g])
              for g in hs]
        v_new = [uw[g][:, :D_HEAD] - ws[g][:c_len] for g in hs]
        v16 = [v_new[g].astype(BF16) for g in hs]
        o = [ws[g][c_len:] + _dot(qk[g].astype(BF16), v16[g]) for g in hs]
        kg_t = [(k[g] * jnp.exp(glast[g] - gcol[g])).T.astype(BF16) for g in hs]
        s_new = [s_old[g] * jnp.exp(glast[g]) + _dot(kg_t[g], v16[g]) for g in hs]
        out = [_rms(o[g], gn) for g in hs]
        if sample:
            out = [out[g][0:tv] * (zz[g] * _sigmoid(zz[g])) for g in hs]
            o_ref[0] = jnp.concatenate(out, axis=1)
        else:
            out = [out[g] * (zz[g] * _sigmoid(zz[g])) for g in hs]
            o_ref[rs, :] = jnp.concatenate(out, axis=1).astype(o_ref.dtype)
        for g in hs:
            s_ref[g] = s_new[g]
        return carry

    lax.fori_loop(0, tb // c_len, chunk, 0)

    @pl.when(ti == n_tb - 1)
    def _():
        rec_ref[0] = s_ref[...]


def _gdn_scratch(tb):
    g = GDN_HEADS_PER_STEP
    gw = g * D_HEAD
    return [pltpu.VMEM((3, tb + SUBLANES, gw), F32),
            pltpu.VMEM((3, tb, gw), F32),
            pltpu.VMEM((2, tb, D_HEAD), F32),
            pltpu.VMEM((g, D_HEAD, D_HEAD), F32)]


def _gdn_params(a_log, dt_bias, n_heads):
    g = GDN_HEADS_PER_STEP
    ng = n_heads // g
    par = jnp.zeros((ng, SUBLANES, D_HEAD), F32)
    par = par.at[:, 0, :g].set(a_log.reshape(ng, g).astype(F32))
    par = par.at[:, 1, :g].set(dt_bias.reshape(ng, g).astype(F32))
    return par


def _gdn_prompt(qkvz, gates, gate_col, w_conv, par, gn, *, nb, t, n_heads):
    g = GDN_HEADS_PER_STEP
    gw = g * D_HEAD
    ng = n_heads // g
    tb = GDN_PROMPT_ROWS
    n_tb = t // tb
    hd = n_heads * D_HEAD
    m = qkvz.shape[0]

    def cur(part):
        return pl.BlockSpec((tb, gw), lambda b, hg, ti: (b * n_tb + ti, part * ng + hg))

    def prev(part):
        return pl.BlockSpec(
            (SUBLANES, gw),
            lambda b, hg, ti: (jnp.maximum((b * n_tb + ti) * (tb // SUBLANES) - 1, 0), part * ng + hg))

    def wspec(part):
        return pl.BlockSpec((CONV_W, gw), lambda b, hg, ti: (0, part * ng + hg))

    in_specs = [cur(0), cur(1), cur(2), cur(3), prev(0), prev(1), prev(2),
                pl.BlockSpec((tb, D_HEAD), lambda b, hg, ti: (b * n_tb + ti, gate_col + hg)),
                wspec(0), wspec(1), wspec(2),
                pl.BlockSpec((1, SUBLANES, D_HEAD), lambda b, hg, ti: (hg, 0, 0)),
                pl.BlockSpec((1, D_HEAD), lambda b, hg, ti: (0, 0))]
    out_shape = (jax.ShapeDtypeStruct((m, 2 * hd), BF16),
                 jax.ShapeDtypeStruct((nb, n_heads, D_HEAD, D_HEAD), F32))
    out_specs = (pl.BlockSpec((tb, gw), lambda b, hg, ti: (b * n_tb + ti, hg)),
                 pl.BlockSpec((1, g, D_HEAD, D_HEAD), lambda b, hg, ti: (b, hg, 0, 0)))
    return pl.pallas_call(
        functools.partial(_gdn_body, tb=tb, tv=tb, sample=False, n_tb=n_tb),
        out_shape=out_shape,
        grid=(nb, ng, n_tb),
        in_specs=in_specs,
        out_specs=out_specs,
        scratch_shapes=_gdn_scratch(tb),
        compiler_params=_cparams(("parallel", "parallel", "arbitrary")),
        name="gdn_prompt",
    )(qkvz, qkvz, qkvz, qkvz, qkvz, qkvz, qkvz, gates, w_conv, w_conv, w_conv,
      par, gn.reshape(1, D_HEAD))


def _gdn_sample(x, conv_prev, ab, rec_prev, w_conv, par, gn, *, n_heads):
    nb, t, _ = x.shape
    g = GDN_HEADS_PER_STEP
    gw = g * D_HEAD
    ng = n_heads // g
    tb = GDN_CHUNK
    hd = n_heads * D_HEAD
    assert t <= tb

    def cur(part):
        return pl.BlockSpec((1, t, gw), lambda b, hg, ti: (b, 0, part * ng + hg))

    def prev(part):
        return pl.BlockSpec((1, CONV_W - 1, gw), lambda b, hg, ti: (b, 0, part * ng + hg))

    def wspec(part):
        return pl.BlockSpec((CONV_W, gw), lambda b, hg, ti: (0, part * ng + hg))

    in_specs = [cur(0), cur(1), cur(2), cur(3), prev(0), prev(1), prev(2),
                pl.BlockSpec((1, t, D_HEAD), lambda b, hg, ti: (b, 0, hg)),
                wspec(0), wspec(1), wspec(2),
                pl.BlockSpec((1, SUBLANES, D_HEAD), lambda b, hg, ti: (hg, 0, 0)),
                pl.BlockSpec((1, D_HEAD), lambda b, hg, ti: (0, 0)),
                pl.BlockSpec((1, g, D_HEAD, D_HEAD), lambda b, hg, ti: (b, hg, 0, 0))]
    out_shape = (jax.ShapeDtypeStruct((nb, t, hd), F32),
                 jax.ShapeDtypeStruct((nb, n_heads, D_HEAD, D_HEAD), F32))
    out_specs = (pl.BlockSpec((1, t, gw), lambda b, hg, ti: (b, 0, hg)),
                 pl.BlockSpec((1, g, D_HEAD, D_HEAD), lambda b, hg, ti: (b, hg, 0, 0)))
    return pl.pallas_call(
        functools.partial(_gdn_body, tb=tb, tv=t, sample=True, n_tb=1),
        out_shape=out_shape,
        grid=(nb, ng, 1),
        in_specs=in_specs,
        out_specs=out_specs,
        scratch_shapes=_gdn_scratch(tb),
        compiler_params=_cparams(("parallel", "parallel", "arbitrary")),
        name="gdn_sample",
    )(x, x, x, x, conv_prev, conv_prev, conv_prev, ab, w_conv, w_conv, w_conv,
      par, gn.reshape(1, D_HEAD), rec_prev)


def _layer(l, x, h, cache_sb_k, cache_sb_v, state_gdn_conv, state_gdn_rec, cache_mem_k,
           cache_mem_v, page_table, mem_prompt, w_in, w_conv, a_log, dt_bias, gdn_norm,
           sb_norm, sb_bias, w_out, norm_mem, mem_norm, w_mem_q, w_mem_k, w_mem_v, w_mem_o,
           norm_mlp, w_up, w_down, *, bp, tp, bs, ts):
    m, d_model = x.shape
    mp, ms = bp * tp, bs * ts
    n_heads = w_conv.shape[-1] // (3 * D_HEAD)
    hd = n_heads * D_HEAD
    n_sb = sb_bias.shape[-1]
    sbd = n_sb * D_HEAD
    gh = GDN_HEADS_PER_STEP
    ng = n_heads // gh
    tm = MM_TM if m % MM_TM == 0 else ROW_TILE

    w_in_t = jnp.swapaxes(w_in, 1, 2)[l]
    ab_col = 4 * hd
    sb_col = ab_col + 2 * n_heads
    tn_in = MM_TN_WIDE
    while (4 * hd) % tn_in or (3 * sbd) % tn_in:
        tn_in //= 2
    proj = _matmul_nt(h, w_in_t, [(0, 4 * hd), (sb_col, 3 * sbd)], tm=tm, tn=tn_in,
                      w_single_buffer=True, name="in_proj")
    w_a = w_in_t[ab_col:ab_col + n_heads].reshape(ng, gh, d_model)
    w_b = w_in_t[ab_col + n_heads:sb_col].reshape(ng, gh, d_model)
    w_ab = jnp.concatenate([w_a, w_b, jnp.zeros((ng, D_HEAD - 2 * gh, d_model), w_in_t.dtype)],
                           axis=1).reshape(ng * D_HEAD, d_model)
    gates = _matmul_nt(h, w_ab, [(0, ng * D_HEAD)], tm=tm, tn=ng * D_HEAD, name="in_proj_gates")
    sb0 = 4 * n_heads

    par = _gdn_params(a_log[l], dt_bias[l], n_heads)
    mix, rec_p = _gdn_prompt(proj, gates, 0, w_conv[l], par, gdn_norm[l],
                             nb=bp, t=tp, n_heads=n_heads)
    samp = proj[mp:]
    xs_rows = samp[:, :4 * hd].reshape(bs, ts, 4 * hd)
    ab_s = gates[mp:].reshape(bs, ts, ng * D_HEAD)
    og_s, rec_s = _gdn_sample(xs_rows, state_gdn_conv[l], ab_s, state_gdn_rec[l], w_conv[l], par,
                              gdn_norm[l], n_heads=n_heads)
    conv_p = jnp.stack([lax.slice(proj, ((b + 1) * tp - (CONV_W - 1), 0), ((b + 1) * tp, 3 * hd))
                        for b in range(bp)])
    conv_s = jnp.concatenate([state_gdn_conv[l], xs_rows[:, :, :3 * hd]], axis=1)[:, -(CONV_W - 1):]

    mix = _sb_prompt(proj, mix, sb_bias[l].astype(F32), sb_norm[l], nb=bp, t=tp, n_heads=n_sb,
                     q_col=sb0, k_col=sb0 + n_sb, v_col=sb0 + 2 * n_sb, o_col=n_heads)
    q_s = samp[:, 4 * hd:4 * hd + sbd].reshape(bs, ts, sbd)
    k_s = samp[:, 4 * hd + sbd:4 * hd + 2 * sbd].reshape(bs, ts, sbd)
    v_s = samp[:, 4 * hd + 2 * sbd:].reshape(bs, ts, sbd)
    n_phys = cache_sb_k.shape[1]
    ck = cache_sb_k[l].reshape(n_phys, PAGE_SIZE * n_sb, D_HEAD)
    cv = cache_sb_v[l].reshape(n_phys, PAGE_SIZE * n_sb, D_HEAD)
    os_s = _sb_sample(q_s, k_s, v_s, ck, cv, page_table, sb_bias[l], sb_norm[l])
    mix_s = jnp.concatenate([og_s.reshape(ms, hd), os_s.reshape(ms, sbd)], axis=1).astype(BF16)
    mix = lax.dynamic_update_slice(mix, mix_s, (mp, 0))
    k_p = proj[:mp, 4 * hd + sbd:4 * hd + 2 * sbd].reshape(bp, tp, n_sb, D_HEAD)
    v_p = proj[:mp, 4 * hd + 2 * sbd:4 * hd + 3 * sbd].reshape(bp, tp, n_sb, D_HEAD)

    x = _matmul(mix, w_out[l], d_model, tm=tm, res=x, name="out_proj")

    n_mem = mem_prompt.shape[1]
    md = w_mem_k.shape[-1]
    memn = _rmsnorm(mem_prompt.reshape(bp * n_mem, d_model), mem_norm[l], BF16, n_mem)
    mk = _matmul(memn, w_mem_k[l], md, tm=bp * n_mem, name="mem_k_proj")
    mv = _matmul(memn, w_mem_v[l], md, tm=bp * n_mem, name="mem_v_proj")
    qm = _norm_matmul(x, norm_mem[l], w_mem_q[l], tm=tm, out_dtype=BF16)
    om_p = _mem_attn(qm[:mp].reshape(bp, tp, md), mk.reshape(bp, n_mem, md), mv.reshape(bp, n_mem, md),
                     min(tp, 512), BF16)
    qm_s = jnp.pad(qm[mp:].astype(F32).reshape(bs, ts, md), ((0, 0), (0, SUBLANES - ts), (0, 0)))
    om_s = _mem_attn(qm_s, cache_mem_k[l].reshape(bs, n_mem, md), cache_mem_v[l].reshape(bs, n_mem, md),
                     SUBLANES, F32)[:, :ts]
    om = jnp.concatenate([om_p.reshape(mp, md), om_s.reshape(ms, md).astype(BF16)], axis=0)
    x, h = _matmul_res_norm(om, w_mem_o[l], x, norm_mlp[l], tm=tm // 2 if tm % 32 == 0 else tm)

    d_ff = w_up.shape[-1]
    hid = _matmul(h, w_up[l], d_ff, tm=tm, tn=MM_TN_WIDE, relu2=True, out_dtype=BF16,
                  w_single_buffer=True, name="mlp_up")
    half = d_ff // 2
    x = _matmul(hid, w_down[l], d_model, tm=tm, k=half, res=x, w_single_buffer=True,
                name="mlp_down_lo")
    x = _matmul(hid, w_down[l], d_model, tm=tm, k=half, a_col_block=1, w_row_block=1, res=x,
                w_single_buffer=True, name="mlp_down_hi")

    n_mh = md // D_HEAD
    outs = dict(k_p=k_p, v_p=v_p, k_s=k_s.reshape(bs, ts, n_sb, D_HEAD),
                v_s=v_s.reshape(bs, ts, n_sb, D_HEAD), conv_p=conv_p, conv_s=conv_s,
                rec_p=rec_p, rec_s=rec_s,
                mk=mk.reshape(bp, n_mem, n_mh, D_HEAD), mv=mv.reshape(bp, n_mem, n_mh, D_HEAD))
    return x, outs


def kernel(x_prompt, x_sample, cache_sb_k, cache_sb_v, state_gdn_conv, state_gdn_rec, cache_mem_k,
           cache_mem_v, page_table, mem_prompt, norm_mix, w_in, w_conv, a_log, dt_bias, gdn_norm,
           sb_norm, sb_bias, w_out, norm_mem, mem_norm, w_mem_q, w_mem_k, w_mem_v, w_mem_o,
           norm_mlp, w_up, w_down, norm_final):
    bp, tp, d_model = x_prompt.shape
    bs, ts, _ = x_sample.shape
    mp, ms = bp * tp, bs * ts
    depth = w_in.shape[0]
    h, x = _rmsnorm_join(x_prompt.reshape(mp, d_model), x_sample.reshape(ms, d_model), norm_mix[0])
    per_layer = []
    for l in range(depth):
        if l > 0:
            h = _rmsnorm(x, norm_mix[l], BF16, ROW_TILE)
        x, outs = _layer(
            l, x, h, cache_sb_k, cache_sb_v, state_gdn_conv, state_gdn_rec, cache_mem_k,
            cache_mem_v, page_table, mem_prompt, w_in, w_conv, a_log, dt_bias, gdn_norm,
            sb_norm, sb_bias, w_out, norm_mem, mem_norm, w_mem_q, w_mem_k, w_mem_v, w_mem_o,
            norm_mlp, w_up, w_down, bp=bp, tp=tp, bs=bs, ts=ts)
        per_layer.append(outs)
    tile_p = 4 * ROW_TILE if mp % (4 * ROW_TILE) == 0 else ROW_TILE
    y_p = _rmsnorm(x, norm_final, F32, tile_p, rows=mp).reshape(bp, tp, d_model)
    y_s = _rmsnorm(x, norm_final, F32, ms, row_block_offset=mp // ms, rows=ms).reshape(bs, ts, d_model)

    def stack(name):
        return jnp.stack([o[name] for o in per_layer])

    return (y_p, y_s, stack("k_p"), stack("v_p"), stack("k_s"), stack("v_s"),
            stack("conv_p"), stack("conv_s"), stack("rec_p"), stack("rec_s"),
            stack("mk"), stack("mv"))
```

```python
import functools
import math

import jax
import jax.numpy as jnp
from jax import lax
from jax.experimental import pallas as pl
from jax.experimental.pallas import tpu as pltpu

F32 = jnp.float32
BF16 = jnp.bfloat16

EPS = 1e-6
D_HEAD = 128
SUBLANES = 8
GDN_CHUNK = 64
GDN_HEADS_PER_STEP = 16
GDN_PROMPT_ROWS = 256
CONV_W = 4
PAGE_SIZE = 128
SB_TQ = 512
SB_TK = 128
SB_PAGES_PER_STEP = 8
V7X_VMEM_LIMIT_BYTES = 56 * 1024 * 1024
MM_TN = 512
MM_TN_WIDE = 1024
MM_TM = 640
ROW_TILE = 128


def _cparams(sem):
    return pltpu.CompilerParams(dimension_semantics=sem, vmem_limit_bytes=V7X_VMEM_LIMIT_BYTES)


def _softplus(x):
    return jnp.maximum(x, 0.0) + jnp.log1p(jnp.exp(-jnp.abs(x)))


def _sigmoid(x):
    return 1.0 / (1.0 + jnp.exp(-x))


def _dot(a, b):
    return jnp.dot(a, b, preferred_element_type=F32)


def _dot_nt(a, b):
    return lax.dot_general(a, b, (((1,), (1,)), ((), ())), preferred_element_type=F32)


def _split2(x):
    hi = x.astype(BF16)
    lo = (x - hi.astype(F32)).astype(BF16)
    return hi, lo


def _split3(x):
    hi = x.astype(BF16)
    r = x - hi.astype(F32)
    mid = r.astype(BF16)
    lo = (r - mid.astype(F32)).astype(BF16)
    return hi, mid, lo


def _dot_hp(a, b):
    a_hi, a_lo = _split2(a)
    b_hi, b_lo = _split2(b)
    return _dot(a_hi, b_hi) + (_dot(a_hi, b_lo) + _dot(a_lo, b_hi))


def _rms(x, g):
    y = x * lax.rsqrt(jnp.mean(x * x, axis=-1, keepdims=True) + EPS)
    return y * g


def _rmsnorm_body(x_ref, g_ref, o_ref):
    o_ref[...] = _rms(x_ref[...], g_ref[...]).astype(o_ref.dtype)


def _rmsnorm(x, g, out_dtype, tm, row_block_offset=0, rows=None):
    m, d = x.shape
    rows = m if rows is None else rows
    off = row_block_offset
    assert rows % tm == 0
    return pl.pallas_call(
        _rmsnorm_body,
        out_shape=jax.ShapeDtypeStruct((rows, d), out_dtype),
        grid=(rows // tm,),
        in_specs=[pl.BlockSpec((tm, d), lambda i: (i + off, 0)),
                  pl.BlockSpec((1, d), lambda i: (0, 0))],
        out_specs=pl.BlockSpec((tm, d), lambda i: (i, 0)),
        compiler_params=_cparams(("parallel",)),
        name="rmsnorm",
    )(x, g.reshape(1, d))


def _rmsnorm_join_body(xp_ref, xs_ref, g_ref, h_ref, x_ref, *, n_prompt_blocks):
    i = pl.program_id(0)

    @pl.when(i < n_prompt_blocks)
    def _():
        x = xp_ref[...]
        x_ref[...] = x
        h_ref[...] = _rms(x, g_ref[...]).astype(h_ref.dtype)

    @pl.when(i >= n_prompt_blocks)
    def _():
        x = xs_ref[...]
        x_ref[...] = x
        h_ref[...] = _rms(x, g_ref[...]).astype(h_ref.dtype)


def _rmsnorm_join(xp, xs, g):
    mp, d = xp.shape
    ms = xs.shape[0]
    tm = ms
    assert mp % tm == 0
    npb = mp // tm
    return pl.pallas_call(
        functools.partial(_rmsnorm_join_body, n_prompt_blocks=npb),
        out_shape=(jax.ShapeDtypeStruct((mp + ms, d), BF16),
                   jax.ShapeDtypeStruct((mp + ms, d), F32)),
        grid=(npb + 1,),
        in_specs=[pl.BlockSpec((tm, d), lambda i: (jnp.minimum(i, npb - 1), 0)),
                  pl.BlockSpec((tm, d), lambda i: (0, 0)),
                  pl.BlockSpec((1, d), lambda i: (0, 0))],
        out_specs=(pl.BlockSpec((tm, d), lambda i: (i, 0)),
                   pl.BlockSpec((tm, d), lambda i: (i, 0))),
        compiler_params=_cparams(("arbitrary",)),
        name="rmsnorm_join",
    )(xp, xs, g.reshape(1, d))


def _matmul_body(*refs, relu2, has_res):
    if has_res:
        a_ref, w_ref, r_ref, o_ref, wbf_ref = refs
    else:
        a_ref, w_ref, o_ref, wbf_ref = refs

    @pl.when(pl.program_id(1) == 0)
    def _():
        wbf_ref[...] = w_ref[...].astype(BF16)

    acc = _dot(a_ref[...], wbf_ref[...])
    if relu2:
        acc = jnp.square(jnp.maximum(acc, 0.0))
    if has_res:
        acc = acc + r_ref[...]
    o_ref[...] = acc.astype(o_ref.dtype)


def _matmul(a, w, n, *, tm, k=None, tn=MM_TN, a_col_block=0, w_row_block=0, w_col_block=0,
            res=None, relu2=False, out_dtype=F32, w_single_buffer=False, name="matmul"):
    m = a.shape[0]
    k = a.shape[1] if k is None else k
    while n % tn:
        tn //= 2
    assert m % tm == 0 and tn % D_HEAD == 0
    w_mode = dict(pipeline_mode=pl.Buffered(1)) if w_single_buffer else {}
    in_specs = [pl.BlockSpec((tm, k), lambda j, i: (i, a_col_block)),
                pl.BlockSpec((k, tn), lambda j, i: (w_row_block, j + w_col_block), **w_mode)]
    args = [a, w]
    if res is not None:
        in_specs.append(pl.BlockSpec((tm, tn), lambda j, i: (i, j)))
        args.append(res)
    return pl.pallas_call(
        functools.partial(_matmul_body, relu2=relu2, has_res=res is not None),
        out_shape=jax.ShapeDtypeStruct((m, n), out_dtype),
        grid=(n // tn, m // tm),
        in_specs=in_specs,
        out_specs=pl.BlockSpec((tm, tn), lambda j, i: (i, j)),
        scratch_shapes=[pltpu.VMEM((k, tn), BF16)],
        compiler_params=_cparams(("parallel", "arbitrary")),
        name=name,
    )(*args)


def _norm_matmul_body(x_ref, g_ref, w_ref, o_ref, wbf_ref):
    @pl.when(pl.program_id(0) == 0)
    def _():
        wbf_ref[...] = w_ref[...].astype(BF16)

    h = _rms(x_ref[...], g_ref[...]).astype(BF16)
    o_ref[...] = _dot(h, wbf_ref[...]).astype(o_ref.dtype)


def _norm_matmul(x, g, w, *, tm, out_dtype):
    m, k = x.shape
    n = w.shape[1]
    assert m % tm == 0
    return pl.pallas_call(
        _norm_matmul_body,
        out_shape=jax.ShapeDtypeStruct((m, n), out_dtype),
        grid=(m // tm,),
        in_specs=[pl.BlockSpec((tm, k), lambda i: (i, 0)),
                  pl.BlockSpec((1, k), lambda i: (0, 0)),
                  pl.BlockSpec((k, n), lambda i: (0, 0))],
        out_specs=pl.BlockSpec((tm, n), lambda i: (i, 0)),
        scratch_shapes=[pltpu.VMEM((k, n), BF16)],
        compiler_params=_cparams(("arbitrary",)),
        name="norm_matmul",
    )(x, g.reshape(1, k), w)


def _matmul_res_norm_body(a_ref, w_ref, r_ref, g_ref, x_ref, h_ref, wbf_ref):
    @pl.when(pl.program_id(0) == 0)
    def _():
        wbf_ref[...] = w_ref[...].astype(BF16)

    x = _dot(a_ref[...], wbf_ref[...]) + r_ref[...]
    x_ref[...] = x
    h_ref[...] = _rms(x, g_ref[...]).astype(h_ref.dtype)


def _matmul_res_norm(a, w, res, g, *, tm):
    m, k = a.shape
    n = w.shape[1]
    assert m % tm == 0
    return pl.pallas_call(
        _matmul_res_norm_body,
        out_shape=(jax.ShapeDtypeStruct((m, n), F32), jax.ShapeDtypeStruct((m, n), BF16)),
        grid=(m // tm,),
        in_specs=[pl.BlockSpec((tm, k), lambda i: (i, 0)),
                  pl.BlockSpec((k, n), lambda i: (0, 0)),
                  pl.BlockSpec((tm, n), lambda i: (i, 0)),
                  pl.BlockSpec((1, n), lambda i: (0, 0))],
        out_specs=(pl.BlockSpec((tm, n), lambda i: (i, 0)),
                   pl.BlockSpec((tm, n), lambda i: (i, 0))),
        scratch_shapes=[pltpu.VMEM((k, n), BF16)],
        compiler_params=_cparams(("arbitrary",)),
        name="matmul_res_norm",
    )(a, w, res, g.reshape(1, n))


def _matmul_nt_body(a_ref, w_ref, o_ref, wbf_ref):
    @pl.when(pl.program_id(1) == 0)
    def _():
        wbf_ref[...] = w_ref[...].astype(BF16)

    o_ref[...] = _dot_nt(a_ref[...], wbf_ref[...]).astype(o_ref.dtype)


def _matmul_nt(a, wt, row_segments, *, tm, tn=MM_TN, w_single_buffer=False, name="matmul_nt"):
    m, k = a.shape
    w_mode = dict(pipeline_mode=pl.Buffered(1)) if w_single_buffer else {}
    starts = []
    for first, n_rows in row_segments:
        assert first % SUBLANES == 0 and n_rows % tn == 0
        starts += [first + i * tn for i in range(n_rows // tn)]
    n = tn * len(starts)
    def w_row(j):
        row8 = j * (tn // SUBLANES)
        base, shift = 0, 0
        for first, n_rows in row_segments:
            step = (first - base * tn) - shift
            if step:
                row8 = row8 + jnp.where(j >= base, step // SUBLANES, 0)
            shift += step
            base += n_rows // tn
        return row8 * SUBLANES

    return pl.pallas_call(
        _matmul_nt_body,
        out_shape=jax.ShapeDtypeStruct((m, n), F32),
        grid=(n // tn, m // tm),
        in_specs=[pl.BlockSpec((tm, k), lambda j, i: (i, 0)),
                  pl.BlockSpec((pl.Element(tn), pl.Element(k)), lambda j, i: (w_row(j), 0), **w_mode)],
        out_specs=pl.BlockSpec((tm, tn), lambda j, i: (i, j)),
        scratch_shapes=[pltpu.VMEM((tn, k), BF16)],
        compiler_params=_cparams(("parallel", "arbitrary")),
        name=name,
    )(a, wt)


def _mem_attn_body(q_ref, k_ref, v_ref, o_ref, *, n_heads, scale):
    for h in range(n_heads):
        sl = slice(h * D_HEAD, (h + 1) * D_HEAD)
        q = q_ref[0, :, sl].astype(BF16)
        k = k_ref[0, :, sl].astype(BF16)
        v = v_ref[0, :, sl].astype(BF16)
        s = _dot_nt(q, k) * scale
        s = s - jnp.max(s, axis=-1, keepdims=True)
        e = jnp.exp(s)
        p = e / jnp.sum(e, axis=-1, keepdims=True)
        o_ref[0, :, sl] = _dot(p.astype(BF16), v).astype(o_ref.dtype)


def _mem_attn(q, k, v, tq, out_dtype):
    nb, t, hd = q.shape
    n_mem = k.shape[1]
    return pl.pallas_call(
        functools.partial(_mem_attn_body, n_heads=hd // D_HEAD, scale=D_HEAD ** -0.5),
        out_shape=jax.ShapeDtypeStruct((nb, t, hd), out_dtype),
        grid=(nb, t // tq),
        in_specs=[pl.BlockSpec((1, tq, hd), lambda b, i: (b, i, 0)),
                  pl.BlockSpec((1, n_mem, hd), lambda b, i: (b, 0, 0)),
                  pl.BlockSpec((1, n_mem, hd), lambda b, i: (b, 0, 0))],
        out_specs=pl.BlockSpec((1, tq, hd), lambda b, i: (b, i, 0)),
        compiler_params=_cparams(("parallel", "parallel")),
        name="mem_attn",
    )(q, k, v)


def _later_keys_neg(n):
    r = lax.broadcasted_iota(jnp.int32, (n, n), 0)
    c = lax.broadcasted_iota(jnp.int32, (n, n), 1)
    return jnp.where(r > c, -1.0, 0.0).astype(BF16)


LOG2_E = 1.4426950408889634


def _sb_blocks(zs, valids, c, u_neg, lead_cols=1):
    sps = [jnp.maximum(z, 0.0) + jnp.log2(1.0 + jnp.exp2(-jnp.abs(z))) for z in zs]
    stays = [sp if v is None else jnp.where(v, sp, 0.0) for sp, v in zip(sps, valids)]
    parts = [_split2(st) for st in stays]
    afters = [_dot(hi, u_neg) + _dot(lo, u_neg) for hi, lo in parts]
    out = []
    for z, sp, st, after, v in zip(zs, sps, stays, afters, valids):
        a = jnp.exp2((z - sp) + after + c)
        out.append(a if v is None else jnp.where(v, a, 0.0))
        if lead_cols:
            for j in range(lead_cols):
                c = c + (after[:, j:j + 1] - st[:, j:j + 1])
        else:
            c = c - jnp.sum(st, axis=-1, keepdims=True)
    return out, c


def _sb_prompt_body(bias_ref, q_ref, k_ref, v_ref, g_ref, mix_ref, o_ref, *, tq, tk, scale):
    del mix_ref
    h = pl.program_id(1)
    qi = pl.program_id(2)
    r = tq // tk
    bias = bias_ref[h] * LOG2_E
    scale = scale * LOG2_E
    q = q_ref[...].astype(BF16)
    u = _later_keys_neg(tk)

    def blocks(qrows, row0, j_hi, masks, c, acc):
        n = qrows.shape[0]
        k0s = [pl.multiple_of((j_hi - i) * tk, tk) for i in range(len(masks))]
        zs = [_dot_nt(qrows, k_ref[pl.ds(k0, tk), :].astype(BF16)) * scale + bias for k0 in k0s]
        row = lax.broadcasted_iota(jnp.int32, (n, tk), 0) + (qi * tq + row0)
        col = lax.broadcasted_iota(jnp.int32, (n, tk), 1)
        valids = [((k0 + col) < row) if m else None for k0, m in zip(k0s, masks)]
        ws, c = _sb_blocks(zs, valids, c, u)
        for w, k0 in zip(ws, k0s):
            acc = acc + _dot(w.astype(BF16), v_ref[pl.ds(k0, tk), :].astype(BF16))
        return c, acc

    hr = r // 2
    half = tq // 2
    zc = jnp.zeros((half, 1), F32)
    za = jnp.zeros((half, D_HEAD), F32)
    c_top, acc_top = blocks(q[:half], 0, qi * r + hr - 1, [True] * hr, zc, za)
    c_bot, acc_bot = blocks(q[half:], half, qi * r + r - 1, [True] * hr + [False] * hr, zc, za)
    c = jnp.concatenate([c_top, c_bot], axis=0)
    acc = jnp.concatenate([acc_top, acc_bot], axis=0)

    def body(it, carry):
        return blocks(q, 0, (qi - 1 - it) * r + (r - 1), [False] * r, *carry)

    c, acc = lax.fori_loop(0, qi, body, (c, acc))
    o_ref[...] = _rms(acc, g_ref[...]).astype(o_ref.dtype)


def _sb_prompt(qkv, mix, bias, gain, *, nb, t, n_heads, q_col, k_col, v_col, o_col):
    tq, tk = SB_TQ, SB_TK
    nq = t // tq
    return pl.pallas_call(
        functools.partial(_sb_prompt_body, tq=tq, tk=tk, scale=D_HEAD ** -0.5),
        out_shape=jax.ShapeDtypeStruct(mix.shape, mix.dtype),
        grid=(nb, n_heads, nq),
        in_specs=[pl.BlockSpec(memory_space=pltpu.SMEM),
                  pl.BlockSpec((tq, D_HEAD), lambda b, h, i: (b * nq + i, q_col + h)),
                  pl.BlockSpec((t, D_HEAD), lambda b, h, i: (b, k_col + h)),
                  pl.BlockSpec((t, D_HEAD), lambda b, h, i: (b, v_col + h)),
                  pl.BlockSpec((1, D_HEAD), lambda b, h, i: (0, 0)),
                  pl.BlockSpec(memory_space=pl.ANY)],
        out_specs=pl.BlockSpec((tq, D_HEAD), lambda b, h, i: (b * nq + i, o_col + h)),
        input_output_aliases={5: 0},
        compiler_params=_cparams(("parallel", "parallel", "arbitrary")),
        name="sb_prompt",
    )(bias, qkv, qkv, qkv, gain.reshape(1, D_HEAD), mix)


def _sb_sample_body(pt_ref, q_ref, kn_ref, vn_ref, *rest, t, n_heads, n_steps, pages_per_step, scale):
    del pt_ref
    pps = pages_per_step
    ck_refs = rest[:pps]
    cv_refs = rest[pps:2 * pps]
    bias_ref, g_ref, o_ref, qall_ref, c_ref, acc_ref, new_ref, kv16_ref = rest[2 * pps:]
    s = pl.program_id(1)
    n_pairs = n_heads // 2
    rp = 2 * SUBLANES
    rows = n_pairs * rp
    tk = PAGE_SIZE
    cols = 2 * tk
    rr = lax.broadcasted_iota(jnp.int32, (cols, cols), 0)
    cc = lax.broadcasted_iota(jnp.int32, (cols, cols), 1)
    u2 = jnp.where(jnp.logical_and((rr >> 1) > (cc >> 1), (rr & 1) == (cc & 1)), -1.0, 0.0).astype(BF16)
    qrow = lax.broadcasted_iota(jnp.int32, (rows, cols), 0)
    col = lax.broadcasted_iota(jnp.int32, (rows, cols), 1)
    own = ((qrow >> 3) & 1) == (col & 1)

    def process(get_ks, get_vs, valids):
        qb = [qall_ref[g * rp:(g + 1) * rp, :].astype(BF16) for g in range(n_pairs)]
        bias2 = bias_ref[...] * LOG2_E
        zs = [jnp.concatenate([_dot_nt(qb[g], get_k(g)) for g in range(n_pairs)], axis=0)
              * (scale * LOG2_E) + bias2 for get_k in get_ks]
        ws, c_new = _sb_blocks(zs, valids, c_ref[...], u2, lead_cols=0)
        ws = [w.astype(BF16) for w in ws]
        outs = []
        for g in range(n_pairs):
            sl = slice(g * rp, (g + 1) * rp)
            o = acc_ref[sl, :]
            for w, get_v in zip(ws, get_vs):
                o = o + _dot(w[sl, :], get_v(g))
            outs.append(o)
        c_ref[...] = c_new
        acc_ref[...] = jnp.concatenate(outs, axis=0)

    @pl.when(s == 0)
    def _():
        qall_ref[...] = jnp.zeros(qall_ref.shape, F32)
        for hd in range(n_heads):
            qall_ref[hd * SUBLANES:hd * SUBLANES + t, :] = q_ref[0, :, hd * D_HEAD:(hd + 1) * D_HEAD]
        c_ref[...] = jnp.zeros(c_ref.shape, F32)
        acc_ref[...] = jnp.zeros(acc_ref.shape, F32)
        new_ref[...] = jnp.zeros(new_ref.shape, F32)
        for hd in range(n_heads):
            for j in range(t):
                r = 2 * j + (hd & 1)
                new_ref[0, hd >> 1, r:r + 1, :] = kn_ref[0, j:j + 1, hd * D_HEAD:(hd + 1) * D_HEAD]
                new_ref[1, hd >> 1, r:r + 1, :] = vn_ref[0, j:j + 1, hd * D_HEAD:(hd + 1) * D_HEAD]
        qi = qrow & (SUBLANES - 1)
        valid = jnp.logical_and(own, jnp.logical_and((col >> 1) < qi, qi < t))
        process([lambda g: new_ref[0, g].astype(BF16)], [lambda g: new_ref[1, g].astype(BF16)], [valid])

    @pl.when(s > 0)
    def _():
        def pair(slot, ref):
            kv16_ref[slot] = pltpu.bitcast(ref[0].astype(BF16), jnp.uint32)

            def get(g):
                return pltpu.bitcast(kv16_ref[slot, pl.ds(g, tk, stride=n_pairs), :], BF16)
            return get

        process([pair(p, r) for p, r in enumerate(ck_refs)],
                [pair(pps + p, r) for p, r in enumerate(cv_refs)], [own] * pps)

    @pl.when(s == n_steps - 1)
    def _():
        for hd in range(n_heads):
            o = acc_ref[hd * SUBLANES:hd * SUBLANES + t, :]
            o_ref[0, :, hd * D_HEAD:(hd + 1) * D_HEAD] = _rms(o, g_ref[...])


def _sb_sample(q, k_new, v_new, cache_k, cache_v, page_table, bias, gain):
    nb, t, hd = q.shape
    n_heads = hd // D_HEAD
    n_pages = page_table.shape[1]
    assert t <= SUBLANES and n_heads % 2 == 0
    rows = n_heads * SUBLANES
    bias_rows = jnp.repeat(bias.astype(F32), SUBLANES).reshape(rows, 1)

    pps = SB_PAGES_PER_STEP
    while n_pages % pps:
        pps //= 2
    n_steps = n_pages // pps + 1

    def page_spec(p):
        def idx(b, s, pt):
            return (pt[b, n_pages - 1 - ((jnp.maximum(s, 1) - 1) * pps + p)], 0, 0)
        return pl.BlockSpec((1, PAGE_SIZE * n_heads, D_HEAD), idx)

    tok_spec = pl.BlockSpec((1, t, hd), lambda b, s, pt: (b, 0, 0))
    grid_spec = pltpu.PrefetchScalarGridSpec(
        num_scalar_prefetch=1,
        grid=(nb, n_steps),
        in_specs=[tok_spec, tok_spec, tok_spec]
                 + [page_spec(p) for p in range(pps)] + [page_spec(p) for p in range(pps)]
                 + [pl.BlockSpec((rows, 1), lambda b, s, pt: (0, 0)),
                    pl.BlockSpec((1, D_HEAD), lambda b, s, pt: (0, 0))],
        out_specs=pl.BlockSpec((1, t, hd), lambda b, s, pt: (b, 0, 0)),
        scratch_shapes=[pltpu.VMEM((rows, D_HEAD), F32),
                        pltpu.VMEM((rows, 1), F32),
                        pltpu.VMEM((rows, D_HEAD), F32),
                        pltpu.VMEM((2, n_heads // 2, 2 * PAGE_SIZE, D_HEAD), F32),
                        pltpu.VMEM((2 * pps, PAGE_SIZE * n_heads // 2, D_HEAD), jnp.uint32)],
    )
    return pl.pallas_call(
        functools.partial(_sb_sample_body, t=t, n_heads=n_heads, n_steps=n_steps,
                          pages_per_step=pps, scale=D_HEAD ** -0.5),
        out_shape=jax.ShapeDtypeStruct((nb, t, hd), F32),
        grid_spec=grid_spec,
        compiler_params=_cparams(("parallel", "arbitrary")),
        name="sb_sample",
    )(page_table, q, k_new, v_new, *([cache_k] * pps), *([cache_v] * pps), bias_rows,
      gain.reshape(1, D_HEAD))


def _unit_lower_inverses(ms, eye):
    ps = [-m for m in ms]
    ts = [eye + p for p in ps]
    for rnd in range(int(math.log2(ms[0].shape[0])) - 1):
        if rnd == 0:
            ps = [_dot_hp(p, p) for p in ps]
            ts = [t + _dot_hp(t, p) for t, p in zip(ts, ps)]
        else:
            p16 = [p.astype(BF16) for p in ps]
            ps = [_dot(p, p) for p in p16]
            ts = [t + _dot(t.astype(BF16), p.astype(BF16)) for t, p in zip(ts, ps)]
    return ts


def _gdn_body(*refs, tb, tv, sample, n_tb):
    g_heads = GDN_HEADS_PER_STEP
    gw = g_heads * D_HEAD
    c_len = GDN_CHUNK
    if sample:
        (xq_ref, xk_ref, xv_ref, z_ref, pq_ref, pk_ref, pv_ref, ab_ref, wq_ref, wk_ref, wv_ref,
         par_ref, gn_ref, s0_ref, o_ref, rec_ref, ext_ref, qkv_ref, gate_ref, s_ref) = refs
    else:
        (xq_ref, xk_ref, xv_ref, z_ref, pq_ref, pk_ref, pv_ref, ab_ref, wq_ref, wk_ref, wv_ref,
         par_ref, gn_ref, o_ref, rec_ref, ext_ref, qkv_ref, gate_ref, s_ref) = refs
    ti = pl.program_id(2)

    @pl.when(ti == 0)
    def _():
        if sample:
            s_ref[...] = s0_ref[0]
        else:
            s_ref[...] = jnp.zeros(s_ref.shape, F32)

    for part, (x_ref, p_ref, w_ref) in enumerate(((xq_ref, pq_ref, wq_ref),
                                                   (xk_ref, pk_ref, wk_ref),
                                                   (xv_ref, pv_ref, wv_ref))):
        if sample:
            ext_ref[part, 0:SUBLANES, :] = jnp.zeros((SUBLANES, gw), F32)
            ext_ref[part, SUBLANES - (CONV_W - 1):SUBLANES, :] = p_ref[0]
            ext_ref[part, SUBLANES:SUBLANES + tb, :] = jnp.zeros((tb, gw), F32)
            ext_ref[part, SUBLANES:SUBLANES + tv, :] = x_ref[0]
        else:
            ext_ref[part, 0:SUBLANES, :] = jnp.where(ti > 0, p_ref[...], 0.0)
            ext_ref[part, SUBLANES:SUBLANES + tb, :] = x_ref[...]
        base = SUBLANES - (CONV_W - 1)
        for g in range(g_heads):
            sl = slice(g * D_HEAD, (g + 1) * D_HEAD)
            w = w_ref[:, sl]
            for r0 in range(0, tb, c_len):
                blk = ext_ref[part, r0:r0 + c_len + SUBLANES, sl]
                n_blk = c_len + SUBLANES
                y = pltpu.roll(blk, n_blk - base, 0)[:c_len] * w[0:1, :]
                for i in range(1, CONV_W - 1):
                    y = y + pltpu.roll(blk, n_blk - base - i, 0)[:c_len] * w[i:i + 1, :]
                y = y + blk[SUBLANES:] * w[CONV_W - 1:CONV_W, :]
                y = y / (1.0 + jnp.exp2(y * (-LOG2_E)))
                if part < 2:
                    inv = lax.rsqrt(jnp.sum(y * y, axis=-1, keepdims=True) + EPS)
                    if part == 0:
                        inv = inv * (D_HEAD ** -0.5)
                    y = y * inv
                qkv_ref[part, r0:r0 + c_len, sl] = y

    if sample:
        gate_ref[0] = jnp.zeros((tb, D_HEAD), F32)
        gate_ref[0, 0:tv, :] = ab_ref[0]
        ab = gate_ref[0]
    else:
        ab = ab_ref[...]
    a_log = par_ref[0, 0:1, :]
    dt_bias = par_ref[0, 1:2, :]
    rowi = lax.broadcasted_iota(jnp.int32, (tb, D_HEAD), 0)
    lane = lax.broadcasted_iota(jnp.int32, (tb, D_HEAD), 1)
    live = rowi < tv
    gdec = -jnp.exp(a_log) * _softplus(ab + dt_bias)
    gate_ref[0] = jnp.where(jnp.logical_and(live, lane < g_heads), gdec, 0.0)
    gate_ref[1] = jnp.where(live, _sigmoid(ab), 0.0)

    ri = lax.broadcasted_iota(jnp.int32, (c_len, c_len), 0)
    ci = lax.broadcasted_iota(jnp.int32, (c_len, c_len), 1)
    incl = ri >= ci
    strict = ri > ci
    eye = jnp.where(ri == ci, 1.0, 0.0).astype(F32)
    tri = jnp.where(ri <= ci, 1.0, 0.0).astype(BF16)
    gn = gn_ref[...]

    def chunk(ck, carry):
        r0 = pl.multiple_of(ck * c_len, c_len)
        rs = pl.ds(r0, c_len)
        g_t = gate_ref[0, rs, :].T
        hi, mid, lo = _split3(g_t)
        gc_t = _dot(hi, tri) + (_dot(mid, tri) + _dot(lo, tri))
        gc = gc_t.T
        beta_all = gate_ref[1, rs, :]
        hs = range(g_heads)
        sls = [slice(g * D_HEAD, (g + 1) * D_HEAD) for g in hs]
        q = [qkv_ref[0, rs, sl] for sl in sls]
        k = [qkv_ref[1, rs, sl] for sl in sls]
        v = [qkv_ref[2, rs, sl] for sl in sls]
        s_old = [s_ref[g] for g in hs]
        if sample:
            zz = [z_ref[0, :, sl] for sl in sls]
        else:
            zz = [z_ref[rs, sl] for sl in sls]
        gcol = [gc[:, g:g + 1] for g in hs]
        grow = [gc_t[g:g + 1, :] for g in hs]
        glast = [gcol[g][c_len - 1:c_len, :] for g in hs]
        beta = [beta_all[:, g_heads + g:g_heads + g + 1] for g in hs]
        decay = [jnp.exp(jnp.where(incl, gcol[g] - grow[g], -jnp.inf)) for g in hs]
        eg = [jnp.exp(gcol[g]) for g in hs]
        kb = [k[g] * beta[g] for g in hs]
        vb = [v[g] * beta[g] for g in hs]
        k16 = [k[g].astype(BF16) for g in hs]
        m = [jnp.where(strict, _dot_nt(kb[g].astype(BF16), k16[g]) * decay[g], 0.0) for g in hs]
        qk = [jnp.where(incl, _dot_nt(q[g].astype(BF16), k16[g]) * decay[g], 0.0) for g in hs]
        tmat = _unit_lower_inverses(m, eye)
        rhs = [jnp.concatenate([vb[g], kb[g] * eg[g]], axis=1).astype(BF16) for g in hs]
        uw = [_dot(tmat[g].astype(BF16), rhs[g]) for g in hs]
        s16 = [s_old[g].astype(BF16) for g in hs]
        ws = [_dot(jnp.concatenate([uw[g][:, D_HEAD:], q[g] * eg[g]], axis=0).astype(BF16), s16[g])
              for g in hs]
        v_new = [uw[g][:, :D_HEAD] - ws[g][:c_len] for g in hs]
        v16 = [v_new[g].astype(BF16) for g in hs]
        o = [ws[g][c_len:] + _dot(qk[g].astype(BF16), v16[g]) for g in hs]
        kg_t = [(k[g] * jnp.exp(glast[g] - gcol[g])).T.astype(BF16) for g in hs]
        s_new = [s_old[g] * jnp.exp(glast[g]) + _dot(kg_t[g], v16[g]) for g in hs]
        out = [_rms(o[g], gn) for g in hs]
        if sample:
            out = [out[g][0:tv] * (zz[g] * _sigmoid(zz[g])) for g in hs]
            o_ref[0] = jnp.concatenate(out, axis=1)
        else:
            out = [out[g] * (zz[g] * _sigmoid(zz[g])) for g in hs]
            o_ref[rs, :] = jnp.concatenate(out, axis=1).astype(o_ref.dtype)
        for g in hs:
            s_ref[g] = s_new[g]
        return carry

    lax.fori_loop(0, tb // c_len, chunk, 0)

    @pl.when(ti == n_tb - 1)
    def _():
        rec_ref[0] = s_ref[...]


def _gdn_scratch(tb):
    g = GDN_HEADS_PER_STEP
    gw = g * D_HEAD
    return [pltpu.VMEM((3, tb + SUBLANES, gw), F32),
            pltpu.VMEM((3, tb, gw), F32),
            pltpu.VMEM((2, tb, D_HEAD), F32),
            pltpu.VMEM((g, D_HEAD, D_HEAD), F32)]


def _gdn_params(a_log, dt_bias, n_heads):
    g = GDN_HEADS_PER_STEP
    ng = n_heads // g
    par = jnp.zeros((ng, SUBLANES, D_HEAD), F32)
    par = par.at[:, 0, :g].set(a_log.reshape(ng, g).astype(F32))
    par = par.at[:, 1, :g].set(dt_bias.reshape(ng, g).astype(F32))
    return par


def _gdn_prompt(qkvz, gates, gate_col, w_conv, par, gn, *, nb, t, n_heads):
    g = GDN_HEADS_PER_STEP
    gw = g * D_HEAD
    ng = n_heads // g
    tb = GDN_PROMPT_ROWS
    n_tb = t // tb
    hd = n_heads * D_HEAD
    m = qkvz.shape[0]

    def cur(part):
        return pl.BlockSpec((tb, gw), lambda b, hg, ti: (b * n_tb + ti, part * ng + hg))

    def prev(part):
        return pl.BlockSpec(
            (SUBLANES, gw),
            lambda b, hg, ti: (jnp.maximum((b * n_tb + ti) * (tb // SUBLANES) - 1, 0), part * ng + hg))

    def wspec(part):
        return pl.BlockSpec((CONV_W, gw), lambda b, hg, ti: (0, part * ng + hg))

    in_specs = [cur(0), cur(1), cur(2), cur(3), prev(0), prev(1), prev(2),
                pl.BlockSpec((tb, D_HEAD), lambda b, hg, ti: (b * n_tb + ti, gate_col + hg)),
                wspec(0), wspec(1), wspec(2),
                pl.BlockSpec((1, SUBLANES, D_HEAD), lambda b, hg, ti: (hg, 0, 0)),
                pl.BlockSpec((1, D_HEAD), lambda b, hg, ti: (0, 0))]
    out_shape = (jax.ShapeDtypeStruct((m, 2 * hd), BF16),
                 jax.ShapeDtypeStruct((nb, n_heads, D_HEAD, D_HEAD), F32))
    out_specs = (pl.BlockSpec((tb, gw), lambda b, hg, ti: (b * n_tb + ti, hg)),
                 pl.BlockSpec((1, g, D_HEAD, D_HEAD), lambda b, hg, ti: (b, hg, 0, 0)))
    return pl.pallas_call(
        functools.partial(_gdn_body, tb=tb, tv=tb, sample=False, n_tb=n_tb),
        out_shape=out_shape,
        grid=(nb, ng, n_tb),
        in_specs=in_specs,
        out_specs=out_specs,
        scratch_shapes=_gdn_scratch(tb),
        compiler_params=_cparams(("parallel", "parallel", "arbitrary")),
        name="gdn_prompt",
    )(qkvz, qkvz, qkvz, qkvz, qkvz, qkvz, qkvz, gates, w_conv, w_conv, w_conv,
      par, gn.reshape(1, D_HEAD))


def _gdn_sample(x, conv_prev, ab, rec_prev, w_conv, par, gn, *, n_heads):
    nb, t, _ = x.shape
    g = GDN_HEADS_PER_STEP
    gw = g * D_HEAD
    ng = n_heads // g
    tb = GDN_CHUNK
    hd = n_heads * D_HEAD
    assert t <= tb

    def cur(part):
        return pl.BlockSpec((1, t, gw), lambda b, hg, ti: (b, 0, part * ng + hg))

    def prev(part):
        return pl.BlockSpec((1, CONV_W - 1, gw), lambda b, hg, ti: (b, 0, part * ng + hg))

    def wspec(part):
        return pl.BlockSpec((CONV_W, gw), lambda b, hg, ti: (0, part * ng + hg))

    in_specs = [cur(0), cur(1), cur(2), cur(3), prev(0), prev(1), prev(2),
                pl.BlockSpec((1, t, D_HEAD), lambda b, hg, ti: (b, 0, hg)),
                wspec(0), wspec(1), wspec(2),
                pl.BlockSpec((1, SUBLANES, D_HEAD), lambda b, hg, ti: (hg, 0, 0)),
                pl.BlockSpec((1, D_HEAD), lambda b, hg, ti: (0, 0)),
                pl.BlockSpec((1, g, D_HEAD, D_HEAD), lambda b, hg, ti: (b, hg, 0, 0))]
    out_shape = (jax.ShapeDtypeStruct((nb, t, hd), F32),
                 jax.ShapeDtypeStruct((nb, n_heads, D_HEAD, D_HEAD), F32))
    out_specs = (pl.BlockSpec((1, t, gw), lambda b, hg, ti: (b, 0, hg)),
                 pl.BlockSpec((1, g, D_HEAD, D_HEAD), lambda b, hg, ti: (b, hg, 0, 0)))
    return pl.pallas_call(
        functools.partial(_gdn_body, tb=tb, tv=t, sample=True, n_tb=1),
        out_shape=out_shape,
        grid=(nb, ng, 1),
        in_specs=in_specs,
        out_specs=out_specs,
        scratch_shapes=_gdn_scratch(tb),
        compiler_params=_cparams(("parallel", "parallel", "arbitrary")),
        name="gdn_sample",
    )(x, x, x, x, conv_prev, conv_prev, conv_prev, ab, w_conv, w_conv, w_conv,
      par, gn.reshape(1, D_HEAD), rec_prev)


def _layer(l, x, h, cache_sb_k, cache_sb_v, state_gdn_conv, state_gdn_rec, cache_mem_k,
           cache_mem_v, page_table, mem_prompt, w_in, w_conv, a_log, dt_bias, gdn_norm,
           sb_norm, sb_bias, w_out, norm_mem, mem_norm, w_mem_q, w_mem_k, w_mem_v, w_mem_o,
           norm_mlp, w_up, w_down, *, bp, tp, bs, ts):
    m, d_model = x.shape
    mp, ms = bp * tp, bs * ts
    n_heads = w_conv.shape[-1] // (3 * D_HEAD)
    hd = n_heads * D_HEAD
    n_sb = sb_bias.shape[-1]
    sbd = n_sb * D_HEAD
    gh = GDN_HEADS_PER_STEP
    ng = n_heads // gh
    tm = MM_TM if m % MM_TM == 0 else ROW_TILE

    w_in_t = jnp.swapaxes(w_in, 1, 2)[l]
    ab_col = 4 * hd
    sb_col = ab_col + 2 * n_heads
    tn_in = MM_TN_WIDE
    while (4 * hd) % tn_in or (3 * sbd) % tn_in:
        tn_in //= 2
    proj = _matmul_nt(h, w_in_t, [(0, 4 * hd), (sb_col, 3 * sbd)], tm=tm, tn=tn_in,
                      w_single_buffer=True, name="in_proj")
    w_a = w_in_t[ab_col:ab_col + n_heads].reshape(ng, gh, d_model)
    w_b = w_in_t[ab_col + n_heads:sb_col].reshape(ng, gh, d_model)
    w_ab = jnp.concatenate([w_a, w_b, jnp.zeros((ng, D_HEAD - 2 * gh, d_model), w_in_t.dtype)],
                           axis=1).reshape(ng * D_HEAD, d_model)
    gates = _matmul_nt(h, w_ab, [(0, ng * D_HEAD)], tm=tm, tn=ng * D_HEAD, name="in_proj_gates")
    sb0 = 4 * n_heads

    par = _gdn_params(a_log[l], dt_bias[l], n_heads)
    mix, rec_p = _gdn_prompt(proj, gates, 0, w_conv[l], par, gdn_norm[l],
                             nb=bp, t=tp, n_heads=n_heads)
    samp = proj[mp:]
    xs_rows = samp[:, :4 * hd].reshape(bs, ts, 4 * hd)
    ab_s = gates[mp:].reshape(bs, ts, ng * D_HEAD)
    og_s, rec_s = _gdn_sample(xs_rows, state_gdn_conv[l], ab_s, state_gdn_rec[l], w_conv[l], par,
                              gdn_norm[l], n_heads=n_heads)
    conv_p = jnp.stack([lax.slice(proj, ((b + 1) * tp - (CONV_W - 1), 0), ((b + 1) * tp, 3 * hd))
                        for b in range(bp)])
    conv_s = jnp.concatenate([state_gdn_conv[l], xs_rows[:, :, :3 * hd]], axis=1)[:, -(CONV_W - 1):]

    mix = _sb_prompt(proj, mix, sb_bias[l].astype(F32), sb_norm[l], nb=bp, t=tp, n_heads=n_sb,
                     q_col=sb0, k_col=sb0 + n_sb, v_col=sb0 + 2 * n_sb, o_col=n_heads)
    q_s = samp[:, 4 * hd:4 * hd + sbd].reshape(bs, ts, sbd)
    k_s = samp[:, 4 * hd + sbd:4 * hd + 2 * sbd].reshape(bs, ts, sbd)
    v_s = samp[:, 4 * hd + 2 * sbd:].reshape(bs, ts, sbd)
    n_phys = cache_sb_k.shape[1]
    ck = cache_sb_k[l].reshape(n_phys, PAGE_SIZE * n_sb, D_HEAD)
    cv = cache_sb_v[l].reshape(n_phys, PAGE_SIZE * n_sb, D_HEAD)
    os_s = _sb_sample(q_s, k_s, v_s, ck, cv, page_table, sb_bias[l], sb_norm[l])
    mix_s = jnp.concatenate([og_s.reshape(ms, hd), os_s.reshape(ms, sbd)], axis=1).astype(BF16)
    mix = lax.dynamic_update_slice(mix, mix_s, (mp, 0))
    k_p = proj[:mp, 4 * hd + sbd:4 * hd + 2 * sbd].reshape(bp, tp, n_sb, D_HEAD)
    v_p = proj[:mp, 4 * hd + 2 * sbd:4 * hd + 3 * sbd].reshape(bp, tp, n_sb, D_HEAD)

    x = _matmul(mix, w_out[l], d_model, tm=tm, res=x, name="out_proj")

    n_mem = mem_prompt.shape[1]
    md = w_mem_k.shape[-1]
    memn = _rmsnorm(mem_prompt.reshape(bp * n_mem, d_model), mem_norm[l], BF16, n_mem)
    mk = _matmul(memn, w_mem_k[l], md, tm=bp * n_mem, name="mem_k_proj")
    mv = _matmul(memn, w_mem_v[l], md, tm=bp * n_mem, name="mem_v_proj")
    qm = _norm_matmul(x, norm_mem[l], w_mem_q[l], tm=tm, out_dtype=BF16)
    om_p = _mem_attn(qm[:mp].reshape(bp, tp, md), mk.reshape(bp, n_mem, md), mv.reshape(bp, n_mem, md),
                     min(tp, 512), BF16)
    qm_s = jnp.pad(qm[mp:].astype(F32).reshape(bs, ts, md), ((0, 0), (0, SUBLANES - ts), (0, 0)))
    om_s = _mem_attn(qm_s, cache_mem_k[l].reshape(bs, n_mem, md), cache_mem_v[l].reshape(bs, n_mem, md),
                     SUBLANES, F32)[:, :ts]
    om = jnp.concatenate([om_p.reshape(mp, md), om_s.reshape(ms, md).astype(BF16)], axis=0)
    x, h = _matmul_res_norm(om, w_mem_o[l], x, norm_mlp[l], tm=tm // 2 if tm % 32 == 0 else tm)

    d_ff = w_up.shape[-1]
    hid = _matmul(h, w_up[l], d_ff, tm=tm, tn=MM_TN_WIDE, relu2=True, out_dtype=BF16,
                  w_single_buffer=True, name="mlp_up")
    half = d_ff // 2
    x = _matmul(hid, w_down[l], d_model, tm=tm, k=half, res=x, w_single_buffer=True,
                name="mlp_down_lo")
    x = _matmul(hid, w_down[l], d_model, tm=tm, k=half, a_col_block=1, w_row_block=1, res=x,
                w_single_buffer=True, name="mlp_down_hi")

    n_mh = md // D_HEAD
    outs = dict(k_p=k_p, v_p=v_p, k_s=k_s.reshape(bs, ts, n_sb, D_HEAD),
                v_s=v_s.reshape(bs, ts, n_sb, D_HEAD), conv_p=conv_p, conv_s=conv_s,
                rec_p=rec_p, rec_s=rec_s,
                mk=mk.reshape(bp, n_mem, n_mh, D_HEAD), mv=mv.reshape(bp, n_mem, n_mh, D_HEAD))
    return x, outs


def kernel(x_prompt, x_sample, cache_sb_k, cache_sb_v, state_gdn_conv, state_gdn_rec, cache_mem_k,
           cache_mem_v, page_table, mem_prompt, norm_mix, w_in, w_conv, a_log, dt_bias, gdn_norm,
           sb_norm, sb_bias, w_out, norm_mem, mem_norm, w_mem_q, w_mem_k, w_mem_v, w_mem_o,
           norm_mlp, w_up, w_down, norm_final):
    bp, tp, d_model = x_prompt.shape
    bs, ts, _ = x_sample.shape
    mp, ms = bp * tp, bs * ts
    depth = w_in.shape[0]
    h, x = _rmsnorm_join(x_prompt.reshape(mp, d_model), x_sample.reshape(ms, d_model), norm_mix[0])
    per_layer = []
    for l in range(depth):
        if l > 0:
            h = _rmsnorm(x, norm_mix[l], BF16, ROW_TILE)
        x, outs = _layer(
            l, x, h, cache_sb_k, cache_sb_v, state_gdn_conv, state_gdn_rec, cache_mem_k,
            cache_mem_v, page_table, mem_prompt, w_in, w_conv, a_log, dt_bias, gdn_norm,
            sb_norm, sb_bias, w_out, norm_mem, mem_norm, w_mem_q, w_mem_k, w_mem_v, w_mem_o,
            norm_mlp, w_up, w_down, bp=bp, tp=tp, bs=bs, ts=ts)
        per_layer.append(outs)
    tile_p = 4 * ROW_TILE if mp % (4 * ROW_TILE) == 0 else ROW_TILE
    y_p = _rmsnorm(x, norm_final, F32, tile_p, rows=mp).reshape(bp, tp, d_model)
    y_s = _rmsnorm(x, norm_final, F32, ms, row_block_offset=mp // ms, rows=ms).reshape(bs, ts, d_model)

    def stack(name):
        return jnp.stack([o[name] for o in per_layer])

    return (y_p, y_s, stack("k_p"), stack("v_p"), stack("k_s"), stack("v_s"),
            stack("conv_p"), stack("conv_s"), stack("rec_p"), stack("rec_s"),
            stack("mk"), stack("mv"))
```
